```python
import math
import jax, jax.numpy as jnp
from jax import lax
import numpy as np

D_MODEL = 1024
BATCH = 4
SEQ = 4096
DEPTH = 1

CTX_LEN = 256
GRID_W = 64
D_MIX = 2 * D_MODEL
SSD_WIDTH = D_MIX // 2
SSD_HEAD_DIM = 64
SSD_HEADS = SSD_WIDTH // SSD_HEAD_DIM
SSD_GROUPS = 2
HEADS_PER_GROUP = SSD_HEADS // SSD_GROUPS
SSD_STATE = 128
SSD_CHUNK = 128
CONV_W = 3
GN = SSD_GROUPS * SSD_STATE
CONV_CH = SSD_WIDTH + 2 * GN
GMLP_WIDTH = D_MIX - SSD_WIDTH
GMLP_GROUPS = 8
GMLP_GROUP_DIM = GMLP_WIDTH // GMLP_GROUPS
GMLP_CHUNK = 128
D_IN_PROJ = SSD_WIDTH + CONV_CH + 2 * SSD_HEADS + 2 * GMLP_WIDTH
IN_SPLITS = [SSD_WIDTH, SSD_WIDTH + CONV_CH, SSD_WIDTH + CONV_CH + 2 * SSD_HEADS,
             SSD_WIDTH + CONV_CH + 2 * SSD_HEADS + GMLP_WIDTH]
PEER_KEYS = 128
PEER_EXPERTS = PEER_KEYS * PEER_KEYS
PEER_HEADS = 8
PEER_KEY_DIM = 256
PEER_TOPK = 16
PEER_BLOCK = 128
EPS = 1e-6

kernel_name = 'hybrid_ssd_gmlp_peer_dit_block'


def rmsnorm(x, w):
    x32 = x.astype(jnp.float32)
    y = x32 * lax.rsqrt(jnp.mean(x32 * x32, axis=-1, keepdims=True) + EPS)
    return (y * w.astype(jnp.float32)).astype(x.dtype)


def modulate(h, shift, scale):
    return h * (1.0 + scale) + shift


def sincos_2d(n_tokens, dim):
    rows = n_tokens // GRID_W
    r, col = jnp.meshgrid(jnp.arange(rows, dtype=jnp.float32),
                          jnp.arange(GRID_W, dtype=jnp.float32), indexing='ij')
    nf = dim // 4
    omega = 1.0 / (10000.0 ** (jnp.arange(nf, dtype=jnp.float32) / nf))
    ar = r.reshape(-1)[:, None] * omega
    ac = col.reshape(-1)[:, None] * omega
    return jnp.concatenate([jnp.sin(ar), jnp.cos(ar), jnp.sin(ac), jnp.cos(ac)], axis=-1)


def conv_centred(x, w, b):
    L = x.shape[1]
    pad = CONV_W // 2
    xp = jnp.pad(x, ((0, 0), (pad, pad), (0, 0)))
    return b + sum(xp[:, k:k + L] * w[k] for k in range(CONV_W))


def ssd_chunked(x, dt, A, B, C, h0):
    b, L, H, P = x.shape
    G, N = B.shape[2], B.shape[3]
    R = H // G
    Q = SSD_CHUNK
    nc = L // Q
    f32 = jnp.float32
    xc = x.astype(f32).reshape(b, nc, Q, G, R, P)
    dtc = dt.astype(f32).reshape(b, nc, Q, G, R)
    Bc = B.astype(f32).reshape(b, nc, Q, G, N)
    Cc = C.astype(f32).reshape(b, nc, Q, G, N)
    dA = dtc * A.astype(f32).reshape(G, R)
    Acs = jnp.cumsum(dA, axis=2)
    xdt = xc * dtc[..., None]
    Acs_h = jnp.moveaxis(Acs, 2, -1)
    seg = Acs_h[..., :, None] - Acs_h[..., None, :]
    tri = jnp.tril(jnp.ones((Q, Q), dtype=bool))
    Lmat = jnp.exp(jnp.where(tri, seg, -jnp.inf))
    CB = jnp.einsum('bctgn,bcsgn->bcgts', Cc, Bc)
    y_diag = jnp.einsum('bcgts,bcgrts,bcsgrp->bctgrp', CB, Lmat, xdt)
    decay_to_end = jnp.exp(Acs[:, :, -1:] - Acs)
    states = jnp.einsum('bcsgn,bcsgr,bcsgrp->bcgrpn', Bc, decay_to_end, xdt)
    chunk_decay = jnp.exp(Acs[:, :, -1])

    def step(h, inp):
        st, dec = inp
        return h * dec[..., None, None] + st, h

    h_final, h_prev = lax.scan(step, h0.astype(f32),
                               (jnp.moveaxis(states, 1, 0), jnp.moveaxis(chunk_decay, 1, 0)))
    h_prev = jnp.moveaxis(h_prev, 0, 1)
    y_off = jnp.einsum('bctgn,bcgrpn,bctgr->bctgrp', Cc, h_prev, jnp.exp(Acs))
    y = (y_diag + y_off).reshape(b, L, H, P)
    return y.astype(x.dtype), h_final


def chunk_gmlp(u, v, w_s, b_s, v_norm_w):
    b, L, _ = u.shape
    nch = L // GMLP_CHUNK
    u = jax.nn.gelu(u).reshape(b, nch, GMLP_CHUNK, GMLP_GROUPS, GMLP_GROUP_DIM)
    v = rmsnorm(jax.nn.gelu(v).reshape(b, L, GMLP_GROUPS, GMLP_GROUP_DIM), v_norm_w)
    vc = v.reshape(b, nch, GMLP_CHUNK, GMLP_GROUPS, GMLP_GROUP_DIM)
    mixed = jnp.einsum('gts,bnsgc->bntgc', w_s, vc) + b_s.T[None, None, :, :, None]
    return (u * mixed).reshape(b, L, GMLP_WIDTH)


def mixer(h, h0_f, h0_b, w_in, conv_w, conv_b, dt_bias, a_log, d_skip, ssd_norm_w,
          gmlp_norm_w, gmlp_ws, gmlp_bs):
    b, L, _ = h.shape
    proj = h @ w_in
    z, xbc, dt_raw, u, v = jnp.split(proj, IN_SPLITS, axis=-1)
    xbc = jax.nn.silu(conv_centred(xbc, conv_w, conv_b))
    xs, bm, cm = jnp.split(xbc, [SSD_WIDTH, SSD_WIDTH + GN], axis=-1)
    xs = xs.reshape(b, L, SSD_HEADS, SSD_HEAD_DIM)
    bm = bm.reshape(b, L, SSD_GROUPS, SSD_STATE)
    cm = cm.reshape(b, L, SSD_GROUPS, SSD_STATE)
    dt = jax.nn.softplus(dt_raw.astype(jnp.float32).reshape(b, L, 2, SSD_HEADS)
                         + dt_bias.astype(jnp.float32))
    A = -jnp.exp(a_log.astype(jnp.float32))
    rev = lambda t: jnp.flip(t, axis=1)
    y_f, st_f = ssd_chunked(xs, dt[:, :, 0], A[0], bm, cm, h0_f)
    y_b, st_b = ssd_chunked(rev(xs), rev(dt[:, :, 1]), A[1], rev(bm), rev(cm), h0_b)
    y = y_f + rev(y_b) + d_skip[:, None].astype(xs.dtype) * xs
    y_ssd = rmsnorm(y.reshape(b, L, SSD_WIDTH) * jax.nn.silu(z), ssd_norm_w)
    y_gmlp = chunk_gmlp(u, v, gmlp_ws, gmlp_bs, gmlp_norm_w)
    return jnp.concatenate([y_ssd, y_gmlp], axis=-1), st_f, st_b


def peer(h, w_q, sub_keys, u_emb, v_emb):
    b, L, D = h.shape
    tok = h.reshape((b * L) // PEER_BLOCK, PEER_BLOCK, D)

    def block_fn(xb):
        tb = xb.shape[0]
        q = (xb @ w_q).reshape(tb, PEER_HEADS, 2, PEER_KEY_DIM // 2)
        s = jnp.einsum('thkd,knd->thkn', q, sub_keys).astype(jnp.float32)
        s1, i1 = lax.top_k(s[:, :, 0], PEER_TOPK)
        s2, i2 = lax.top_k(s[:, :, 1], PEER_TOPK)
        cand = (s1[..., :, None] + s2[..., None, :]).reshape(tb, PEER_HEADS, PEER_TOPK * PEER_TOPK)
        cid = (i1[..., :, None] * PEER_KEYS + i2[..., None, :]).reshape(tb, PEER_HEADS, PEER_TOPK * PEER_TOPK)
        top_s, top_j = lax.top_k(cand, PEER_TOPK)
        eid = jnp.take_along_axis(cid, top_j, axis=-1)
        g = jax.nn.softmax(top_s, axis=-1)
        ue = u_emb[eid]
        ve = v_emb[eid]
        act = jax.nn.gelu(jnp.einsum('thkd,td->thk', ue, xb).astype(jnp.float32))
        return jnp.einsum('thk,thkd->td', (g * act).astype(ve.dtype), ve).astype(xb.dtype)

    return lax.map(block_fn, tok).reshape(b, L, D)


def setup_inputs(seed: int = 0) -> dict:
    key = jax.random.key(seed)
    ks = jax.random.split(key, 24)
    f32 = jnp.float32
    nrm = lambda k, shape, s: jax.random.normal(k, shape, f32) * s
    dt0 = jnp.exp(jax.random.uniform(ks[10], (DEPTH, 2, SSD_HEADS), f32, math.log(1e-3), math.log(1e-1)))
    return {
        'x': nrm(ks[0], (BATCH, SEQ, D_MODEL), 1.0),
        'c': nrm(ks[1], (BATCH, D_MODEL), 1.0),
        'ctx': nrm(ks[2], (BATCH, CTX_LEN, D_MODEL), 1.0),
        'c_ctx': nrm(ks[3], (D_MODEL,), 1.0),
        'w_mod': nrm(ks[4], (DEPTH, D_MODEL, 6 * D_MODEL), 0.1 * D_MODEL ** -0.5),
        'b_mod': nrm(ks[5], (DEPTH, 6 * D_MODEL), 0.01),
        'norm1_w': 1.0 + nrm(ks[6], (DEPTH, D_MODEL), 0.01),
        'w_in': nrm(ks[7], (DEPTH, D_MODEL, D_IN_PROJ), D_MODEL ** -0.5),
        'conv_w': nrm(ks[8], (DEPTH, CONV_W, CONV_CH), CONV_W ** -0.5),
        'conv_b': nrm(ks[9], (DEPTH, CONV_CH), 0.01),
        'dt_bias': dt0 + jnp.log(-jnp.expm1(-dt0)),
        'a_log': jnp.log(jax.random.uniform(ks[11], (DEPTH, 2, SSD_HEADS), f32, 1.0, 16.0)),
        'd_skip': 1.0 + nrm(ks[12], (DEPTH, SSD_HEADS), 0.01),
        'ssd_norm_w': 1.0 + nrm(ks[13], (DEPTH, SSD_WIDTH), 0.01),
        'gmlp_norm_w': 1.0 + nrm(ks[14], (DEPTH, GMLP_GROUPS, GMLP_GROUP_DIM), 0.01),
        'gmlp_ws': nrm(ks[15], (DEPTH, GMLP_GROUPS, GMLP_CHUNK, GMLP_CHUNK), GMLP_CHUNK ** -0.5),
        'gmlp_bs': 1.0 + nrm(ks[16], (DEPTH, GMLP_GROUPS, GMLP_CHUNK), 0.01),
        'w_out': nrm(ks[17], (DEPTH, D_MIX, D_MODEL), D_MIX ** -0.5),
        'norm2_w': 1.0 + nrm(ks[18], (DEPTH, D_MODEL), 0.01),
        'peer_wq': nrm(ks[19], (DEPTH, D_MODEL, PEER_HEADS * PEER_KEY_DIM), D_MODEL ** -0.5),
        'peer_keys': nrm(ks[20], (DEPTH, 2, PEER_KEYS, PEER_KEY_DIM // 2), (PEER_KEY_DIM // 2) ** -0.5),
        'peer_u': nrm(ks[21], (DEPTH, PEER_EXPERTS, D_MODEL), D_MODEL ** -0.5),
        'peer_v': nrm(ks[22], (DEPTH, PEER_EXPERTS, D_MODEL), 0.5),
        'final_norm_w': 1.0 + nrm(ks[23], (D_MODEL,), 0.01),
    }


def reference(x, c, ctx, c_ctx, w_mod, b_mod, norm1_w, w_in, conv_w, conv_b, dt_bias, a_log,
              d_skip, ssd_norm_w, gmlp_norm_w, gmlp_ws, gmlp_bs, w_out, norm2_w, peer_wq,
              peer_keys, peer_u, peer_v, final_norm_w):
    b, L, D = x.shape
    x = x + sincos_2d(L, D).astype(x.dtype)[None]
    s = ctx
    for layer in range(DEPTH):
        last = layer == DEPTH - 1
        mod_x = [m[:, None, :] for m in
                 jnp.split(jax.nn.silu(c) @ w_mod[layer] + b_mod[layer], 6, axis=-1)]
        mod_s = jnp.split(jax.nn.silu(c_ctx) @ w_mod[layer] + b_mod[layer], 6, axis=-1)
        mix_params = (w_in[layer], conv_w[layer], conv_b[layer], dt_bias[layer], a_log[layer],
                      d_skip[layer], ssd_norm_w[layer], gmlp_norm_w[layer], gmlp_ws[layer],
                      gmlp_bs[layer])
        peer_params = (peer_wq[layer], peer_keys[layer], peer_u[layer], peer_v[layer])
        zeros = jnp.zeros((b, SSD_GROUPS, HEADS_PER_GROUP, SSD_HEAD_DIM, SSD_STATE), jnp.float32)
        hs = modulate(rmsnorm(s, norm1_w[layer]), mod_s[0], mod_s[1])
        feat_s, st_f, st_b = mixer(hs, zeros, zeros, *mix_params)
        hx = modulate(rmsnorm(x, norm1_w[layer]), mod_x[0], mod_x[1])
        feat_x, _, _ = mixer(hx, st_f, st_b, *mix_params)
        x = x + mod_x[2] * (feat_x @ w_out[layer])
        if not last:
            s = s + mod_s[2] * (feat_s @ w_out[layer])
            hs2 = modulate(rmsnorm(s, norm2_w[layer]), mod_s[3], mod_s[4])
            s = s + mod_s[5] * peer(hs2, *peer_params)
        hx2 = modulate(rmsnorm(x, norm2_w[layer]), mod_x[3], mod_x[4])
        x = x + mod_x[5] * peer(hx2, *peer_params)
    return rmsnorm(x, final_norm_w)
```

```python
import functools
import math

import jax
import jax.numpy as jnp
from jax import lax
from jax.experimental import pallas as pl
from jax.experimental.pallas import tpu as pltpu

F32 = jnp.float32
BF16 = jnp.bfloat16
HI = lax.Precision.HIGHEST

GRID_W = 64
SSD_HEAD_DIM = 64
SSD_HEADS = 16
SSD_GROUPS = 2
HEADS_PER_GROUP = SSD_HEADS // SSD_GROUPS
SSD_STATE = 128
CHUNK = 128
SSD_WIDTH = SSD_HEADS * SSD_HEAD_DIM
GN = SSD_GROUPS * SSD_STATE
CONV_CH = SSD_WIDTH + 2 * GN
GMLP_GROUPS = 8
GMLP_GROUP_DIM = 128
GMLP_WIDTH = GMLP_GROUPS * GMLP_GROUP_DIM
PEER_KEYS = 128
PEER_HEADS = 8
PEER_HALF_DIM = 128
PEER_TOPK = 16
EPS = 1e-6

LANES = 128
SUBLANES = 8
VMEM_LIMIT = 56 * 1024 * 1024

GROUP_W = HEADS_PER_GROUP * SSD_HEAD_DIM


def _cparams(n_axes):
    return pltpu.CompilerParams(
        dimension_semantics=("arbitrary",) * n_axes,
        vmem_limit_bytes=VMEM_LIMIT)


def _rms(x, w):
    return x * lax.rsqrt(jnp.mean(x * x, axis=-1, keepdims=True) + EPS) * w


def _silu(x):
    return x * jax.nn.sigmoid(x)


def _dot(a, b):
    return jnp.dot(a, b, preferred_element_type=F32)


def _dot_nt(a, b):
    return lax.dot_general(a, b, (((1,), (1,)), ((), ())), preferred_element_type=F32)


def _mod_kernel(c_ref, w_ref, b_ref, o_ref):
    a = _silu(c_ref[...]).astype(BF16)
    o_ref[...] = _dot(a, w_ref[...].astype(BF16)) + b_ref[...]


def _mod_call(cc, w_mod, b_mod):
    d, n = w_mod.shape
    bn = 1536
    return pl.pallas_call(
        _mod_kernel,
        grid=(n // bn,),
        in_specs=[pl.BlockSpec((SUBLANES, d), lambda j: (0, 0)),
                  pl.BlockSpec((d, bn), lambda j: (0, j)),
                  pl.BlockSpec((1, bn), lambda j: (0, j))],
        out_specs=pl.BlockSpec((SUBLANES, bn), lambda j: (0, j)),
        out_shape=jax.ShapeDtypeStruct((SUBLANES, n), F32),
        compiler_params=_cparams(1),
        name="mod",
    )(cc, w_mod, b_mod)


def _pos_kernel(omega_ref, o_ref):
    nf = omega_ref.shape[1]
    om = omega_ref[...]
    r = pl.program_id(0).astype(F32)
    col = lax.broadcasted_iota(jnp.int32, (GRID_W, nf), 0).astype(F32)
    ar = jnp.broadcast_to(r * om, (GRID_W, nf))
    ac = col * om
    o_ref[:, 0 * nf:1 * nf] = jnp.sin(ar)
    o_ref[:, 1 * nf:2 * nf] = jnp.cos(ar)
    o_ref[:, 2 * nf:3 * nf] = jnp.sin(ac)
    o_ref[:, 3 * nf:4 * nf] = jnp.cos(ac)


def _pos_call(seq, dim, omega):
    return pl.pallas_call(
        _pos_kernel,
        grid=(seq // GRID_W,),
        in_specs=[pl.BlockSpec((1, dim // 4), lambda i: (0, 0))],
        out_specs=pl.BlockSpec((GRID_W, dim), lambda i: (i, 0)),
        out_shape=jax.ShapeDtypeStruct((seq, dim), F32),
        compiler_params=_cparams(1),
        name="pos",
    )(omega)


def _conv_silu(x, prev_row, next_row, w, b):
    n = x.shape[0]
    rows = lax.broadcasted_iota(jnp.int32, x.shape, 0)
    x_prev = jnp.where(rows == 0, prev_row, pltpu.roll(x, 1, axis=0))
    x_next = jnp.where(rows == n - 1, next_row, pltpu.roll(x, n - 1, axis=0))
    y = b + x_prev * w[0:1, :] + x * w[1:2, :] + x_next * w[2:3, :]
    return _silu(y)


def _softplus(x):
    return jnp.maximum(x, 0.0) + jnp.log1p(jnp.exp(-jnp.abs(x)))


def _ssd_chunk(xbc, dt_all, a_all, expand, h_ref, reverse, want_y):
    q = xbc.shape[0]
    col0 = SSD_HEADS if reverse else 0
    xs = xbc[:, :SSD_WIDTH]
    d_a = dt_all * a_all
    ti = lax.broadcasted_iota(jnp.int32, (q, q), 0)
    si = lax.broadcasted_iota(jnp.int32, (q, q), 1)
    keep = (si >= ti) if reverse else (si <= ti)
    tri = jnp.where(keep, 1.0, 0.0).astype(F32)
    acs = jnp.dot(tri, d_a, precision=HI, preferred_element_type=F32)
    last = 0 if reverse else q - 1
    total = acs[last:last + 1, :]
    stacked = jnp.concatenate([jnp.exp(acs), dt_all, jnp.exp(total - acs)], axis=0)
    ex = jnp.dot(stacked, expand, precision=HI, preferred_element_type=F32)
    ea_x, dt_x, dte_x = ex[:q], ex[q:2 * q], ex[2 * q:]
    xdt = xs * dt_x
    xd = (xdt * dte_x).astype(BF16)
    xdt_b = xdt.astype(BF16)
    chunk_decay = ea_x[last:last + 1, :]
    acs_t = acs.T if want_y else None
    lane = lax.broadcasted_iota(jnp.int32, (q, LANES), 1)
    ys = []
    for g in range(SSD_GROUPS):
        bm = xbc[:, SSD_WIDTH + g * SSD_STATE:SSD_WIDTH + (g + 1) * SSD_STATE]
        cm = xbc[:, SSD_WIDTH + GN + g * SSD_STATE:SSD_WIDTH + GN + (g + 1) * SSD_STATE]
        gsl = slice(g * GROUP_W, (g + 1) * GROUP_W)
        h_old = h_ref[g]
        if want_y:
            cm_b = cm.astype(BF16)
            cb = _dot_nt(cm_b, bm.astype(BF16))
            y_off = _dot(cm_b, h_old.astype(BF16)) * ea_x[:, gsl]
            pieces = []
            for pr in range(HEADS_PER_GROUP // 2):
                ms = []
                for r in (2 * pr, 2 * pr + 1):
                    j = col0 + g * HEADS_PER_GROUP + r
                    seg = acs[:, j:j + 1] - acs_t[j:j + 1, :]
                    lm = jnp.exp(jnp.where(keep, seg, -jnp.inf))
                    ms.append((cb * lm).astype(BF16))
                c0 = g * GROUP_W + pr * LANES
                xp = xdt_b[:, c0:c0 + LANES]
                zero = jnp.zeros_like(xp)
                rhs = jnp.concatenate([jnp.where(lane < SSD_HEAD_DIM, xp, zero),
                                       jnp.where(lane >= SSD_HEAD_DIM, xp, zero)], axis=0)
                pieces.append(_dot(jnp.concatenate(ms, axis=1), rhs))
            ys.append(jnp.concatenate(pieces, axis=1) + y_off)
        h_ref[g] = h_old * chunk_decay[:, gsl] + _dot(bm.T.astype(BF16), xd[:, gsl])
    return jnp.concatenate(ys, axis=1) if want_y else None


def _expand_matrix(reverse):
    rows = lax.broadcasted_iota(jnp.int32, (LANES, SSD_WIDTH), 0)
    cols = lax.broadcasted_iota(jnp.int32, (LANES, SSD_WIDTH), 1)
    head = cols // SSD_HEAD_DIM + (SSD_HEADS if reverse else 0)
    return jnp.where(rows == head, 1.0, 0.0).astype(F32)


def _ctx_kernel(ctx_ref, shift_ref, scale_ref, nw_ref, w_ref, cw_ref, cb_ref, dtb_ref, alog_ref,
                stf_ref, stb_ref):
    h = _rms(ctx_ref[0], nw_ref[...]) * (1.0 + scale_ref[...]) + shift_ref[...]
    proj = _dot(h.astype(BF16), w_ref[...])
    n = proj.shape[0]
    zero_row = jnp.zeros((1, CONV_CH), F32)
    xbc = _conv_silu(proj[:, :CONV_CH], zero_row, zero_row, cw_ref[...], cb_ref[...])
    dt_all = _softplus(proj[:, CONV_CH:] + dtb_ref[...])
    a_all = -jnp.exp(alog_ref[...])
    stf_ref[...] = jnp.zeros_like(stf_ref)
    stb_ref[...] = jnp.zeros_like(stb_ref)
    nck = n // CHUNK
    ef, eb = _expand_matrix(False), _expand_matrix(True)
    for ci in range(nck):
        sl = slice(ci * CHUNK, (ci + 1) * CHUNK)
        _ssd_chunk(xbc[sl], dt_all[sl], a_all, ef, stf_ref.at[0], False, False)
    for ci in reversed(range(nck)):
        sl = slice(ci * CHUNK, (ci + 1) * CHUNK)
        _ssd_chunk(xbc[sl], dt_all[sl], a_all, eb, stb_ref.at[0], True, False)


def _ctx_call(ctx, shift, scale, norm_w, w_xd, conv_w, conv_b, dtb, alog):
    b, n, d = ctx.shape
    st_shape = jax.ShapeDtypeStruct((b, SSD_GROUPS, SSD_STATE, GROUP_W), F32)
    st_spec = pl.BlockSpec((1, SSD_GROUPS, SSD_STATE, GROUP_W), lambda i: (i, 0, 0, 0))
    full = lambda a: pl.BlockSpec(a.shape, lambda i: (0,) * a.ndim)
    return pl.pallas_call(
        _ctx_kernel,
        grid=(b,),
        in_specs=[pl.BlockSpec((1, n, d), lambda i: (i, 0, 0)),
                  full(shift), full(scale), full(norm_w), full(w_xd), full(conv_w), full(conv_b),
                  full(dtb), full(alog)],
        out_specs=(st_spec, st_spec),
        out_shape=(st_shape, st_shape),
        compiler_params=_cparams(1),
        name="ctx_state",
    )(ctx, shift, scale, norm_w, w_xd, conv_w, conv_b, dtb, alog)


def _inproj_kernel(x_ref, pos_ref, shift_ref, scale_ref, nw_ref, w_ref, gw_ref, gnw_ref, gb_ref,
                   xp_ref, z_ref, xbc_ref, dt_ref, yg_ref):
    xp = x_ref[0] + pos_ref[...]
    xp_ref[0] = xp
    h = _rms(xp, nw_ref[...]) * (1.0 + scale_ref[0]) + shift_ref[0]
    proj = _dot(h.astype(BF16), w_ref[...])
    z_ref[0] = proj[:, :SSD_WIDTH]
    xbc_ref[0] = proj[:, SSD_WIDTH:SSD_WIDTH + CONV_CH]
    o = SSD_WIDTH + CONV_CH
    u = jax.nn.gelu(proj[:, o:o + GMLP_WIDTH])
    v = jax.nn.gelu(proj[:, o + GMLP_WIDTH:o + 2 * GMLP_WIDTH])
    dt_ref[0] = proj[:, o + 2 * GMLP_WIDTH:]
    tm = u.shape[0]
    gnw = gnw_ref[...]
    gb = gb_ref[...]
    for g in range(GMLP_GROUPS):
        gs = slice(g * GMLP_GROUP_DIM, (g + 1) * GMLP_GROUP_DIM)
        vn = _rms(v[:, gs], gnw[:, gs]).astype(BF16)
        wg = gw_ref[g]
        for ci in range(tm // CHUNK):
            rs = slice(ci * CHUNK, (ci + 1) * CHUNK)
            mixed = _dot(wg, vn[rs]) + gb[:, gs]
            yg_ref[0, rs, gs] = (u[rs, gs] * mixed).astype(BF16)


def _inproj_call(x, pos, shift, scale, norm_w, w_main, gw, gnw, gb, tm):
    b, seq, d = x.shape
    nt = seq // tm
    full = lambda a: pl.BlockSpec(a.shape, lambda i, j: (0,) * a.ndim)
    tok = lambda w: pl.BlockSpec((1, tm, w), lambda i, j: (i, j, 0))
    per_b = pl.BlockSpec((1, 1, d), lambda i, j: (i, 0, 0))
    shp = lambda w, dt: jax.ShapeDtypeStruct((b, seq, w), dt)
    return pl.pallas_call(
        _inproj_kernel,
        grid=(b, nt),
        in_specs=[tok(d), pl.BlockSpec((tm, d), lambda i, j: (j, 0)), per_b, per_b, full(norm_w),
                  full(w_main), full(gw), full(gnw), full(gb)],
        out_specs=(tok(d), tok(SSD_WIDTH), tok(CONV_CH), tok(LANES), tok(GMLP_WIDTH)),
        out_shape=(shp(d, F32), shp(SSD_WIDTH, F32), shp(CONV_CH, F32), shp(LANES, F32),
                   shp(GMLP_WIDTH, BF16)),
        compiler_params=_cparams(2),
        name="inproj",
    )(x, pos, shift, scale, norm_w, w_main, gw, gnw, gb)


def _halo_rows(prev_ref, next_ref, c, nc):
    prev_row = jnp.where(c > 0, prev_ref[0, SUBLANES - 1:SUBLANES, :], 0.0)
    next_row = jnp.where(c < nc - 1, next_ref[0, 0:1, :], 0.0)
    return prev_row, next_row


def _ssd_fwd_kernel(xbc_ref, prev_ref, next_ref, dt_ref, cw_ref, cb_ref, dtb_ref, alog_ref,
                    dskip_ref, h0_ref, y_ref, h_ref):
    c = pl.program_id(1)
    nc = pl.num_programs(1)

    @pl.when(c == 0)
    def _():
        h_ref[...] = h0_ref[0]

    prev_row, next_row = _halo_rows(prev_ref, next_ref, c, nc)
    xbc = _conv_silu(xbc_ref[0], prev_row, next_row, cw_ref[...], cb_ref[...])
    dt_all = _softplus(dt_ref[0] + dtb_ref[...])
    a_all = -jnp.exp(alog_ref[...])
    y = _ssd_chunk(xbc, dt_all, a_all, _expand_matrix(False), h_ref, False, True)
    y_ref[0] = y + dskip_ref[...] * xbc[:, :SSD_WIDTH]


def _ssd_bwd_kernel(xbc_ref, prev_ref, next_ref, dt_ref, cw_ref, cb_ref, dtb_ref, alog_ref,
                    h0_ref, yf_ref, z_ref, yg_ref, xp_ref, gate_ref, snw_ref, wo_ref,
                    o_ref, h_ref):
    cr = pl.program_id(1)
    nc = pl.num_programs(1)
    c = nc - 1 - cr

    @pl.when(cr == 0)
    def _():
        h_ref[...] = h0_ref[0]

    prev_row, next_row = _halo_rows(prev_ref, next_ref, c, nc)
    xbc = _conv_silu(xbc_ref[0], prev_row, next_row, cw_ref[...], cb_ref[...])
    dt_all = _softplus(dt_ref[0] + dtb_ref[...])
    a_all = -jnp.exp(alog_ref[...])
    y = yf_ref[0] + _ssd_chunk(xbc, dt_all, a_all, _expand_matrix(True), h_ref, True, True)
    y_ssd = _rms(y * _silu(z_ref[0]), snw_ref[...])
    out = (_dot(y_ssd.astype(BF16), wo_ref[:SSD_WIDTH, :])
           + _dot(yg_ref[0], wo_ref[SSD_WIDTH:, :]))
    o_ref[0] = xp_ref[0] + gate_ref[0] * out


def _ssd_specs(seq, reverse):
    nc = seq // CHUNK
    hb = CHUNK // SUBLANES
    last_hb = seq // SUBLANES - 1
    cidx = (lambda j: nc - 1 - j) if reverse else (lambda j: j)
    cur = lambda w: pl.BlockSpec((1, CHUNK, w), lambda i, j: (i, cidx(j), 0))
    prev = pl.BlockSpec((1, SUBLANES, CONV_CH),
                        lambda i, j: (i, jnp.maximum(cidx(j) * hb - 1, 0), 0))
    nxt = pl.BlockSpec((1, SUBLANES, CONV_CH),
                       lambda i, j: (i, jnp.minimum((cidx(j) + 1) * hb, last_hb), 0))
    return nc, cur, prev, nxt


def _ssd_fwd_call(xbc_raw, dt_raw, conv_w, conv_b, dtb, alog, dskip, st_f):
    b, seq, _ = xbc_raw.shape
    nc, cur, prev, nxt = _ssd_specs(seq, False)
    full = lambda a: pl.BlockSpec(a.shape, lambda i, j: (0,) * a.ndim)
    st = pl.BlockSpec((1, SSD_GROUPS, SSD_STATE, GROUP_W), lambda i, j: (i, 0, 0, 0))
    return pl.pallas_call(
        _ssd_fwd_kernel,
        grid=(b, nc),
        in_specs=[cur(CONV_CH), prev, nxt, cur(LANES), full(conv_w), full(conv_b), full(dtb),
                  full(alog), full(dskip), st],
        out_specs=cur(SSD_WIDTH),
        out_shape=jax.ShapeDtypeStruct((b, seq, SSD_WIDTH), F32),
        scratch_shapes=[pltpu.VMEM((SSD_GROUPS, SSD_STATE, GROUP_W), F32)],
        compiler_params=_cparams(2),
        name="ssd_fwd",
    )(xbc_raw, xbc_raw, xbc_raw, dt_raw, conv_w, conv_b, dtb, alog, dskip, st_f)


def _ssd_bwd_call(xbc_raw, dt_raw, conv_w, conv_b, dtb, alog, st_b, y_f, z, y_g, xp, gate, snw, w_out):
    b, seq, d = xp.shape
    nc, cur, prev, nxt = _ssd_specs(seq, True)
    full = lambda a: pl.BlockSpec(a.shape, lambda i, j: (0,) * a.ndim)
    st = pl.BlockSpec((1, SSD_GROUPS, SSD_STATE, GROUP_W), lambda i, j: (i, 0, 0, 0))
    per_b = pl.BlockSpec((1, 1, d), lambda i, j: (i, 0, 0))
    return pl.pallas_call(
        _ssd_bwd_kernel,
        grid=(b, nc),
        in_specs=[cur(CONV_CH), prev, nxt, cur(LANES), full(conv_w), full(conv_b), full(dtb),
                  full(alog), st, cur(SSD_WIDTH), cur(SSD_WIDTH), cur(GMLP_WIDTH), cur(d), per_b,
                  full(snw), full(w_out)],
        out_specs=cur(d),
        out_shape=jax.ShapeDtypeStruct((b, seq, d), F32),
        scratch_shapes=[pltpu.VMEM((SSD_GROUPS, SSD_STATE, GROUP_W), F32)],
        compiler_params=_cparams(2),
        name="ssd_bwd",
    )(xbc_raw, xbc_raw, xbc_raw, dt_raw, conv_w, conv_b, dtb, alog, st_b, y_f, z, y_g, xp, gate,
      snw, w_out)


def _top16_desc(s):
    out = []
    for _ in range(PEER_TOPK):
        m = jnp.max(s, axis=0, keepdims=True)
        out.append(m)
        s = jnp.where(s == m, -jnp.inf, s)
    return out


def _peer_prep_kernel(x_ref, shift_ref, scale_ref, nw_ref, wq_ref, keys_ref,
                      ht_ref, s1_ref, e1_ref, s2_ref, e2_ref, thr_ref):
    h = _rms(x_ref[...], nw_ref[...]) * (1.0 + scale_ref[0]) + shift_ref[0]
    ht_ref[...] = h.T.astype(BF16)
    q = _dot(h.astype(BF16), wq_ref[...])
    tp = q.shape[0]
    neg = jnp.full((SUBLANES, tp), -jnp.inf, F32)
    row8 = lax.broadcasted_iota(jnp.int32, (SUBLANES, tp), 0)
    for hd in range(PEER_HEADS):
        s = []
        for k in range(2):
            o = (hd * 2 + k) * PEER_HALF_DIM
            s.append(_dot_nt(keys_ref[k], q[:, o:o + PEER_HALF_DIM].astype(BF16)))
        a = _top16_desc(s[0])
        b = _top16_desc(s[1])
        b16 = jnp.concatenate(b, axis=0)
        b8 = b16[:SUBLANES]
        cands = [a[0] + b16, a[1] + b8]
        for i in range(2, 8):
            cands.append(jnp.where(row8 < PEER_TOPK // (i + 1), a[i] + b8, neg))
        cands.append(jnp.concatenate(a[8:], axis=0) + b[0])
        cand = jnp.concatenate(cands, axis=0)
        thr = _top16_desc(cand)[-1]
        m = a[0] + b[0]
        zsum = jnp.sum(jnp.where(cand >= thr, jnp.exp(cand - m), 0.0), axis=0, keepdims=True)
        s1_ref[hd] = s[0]
        e1_ref[hd] = jnp.exp(s[0] - a[0]) / zsum
        s2_ref[hd] = s[1]
        e2_ref[hd] = jnp.exp(s[1] - b[0])
        thr_ref[hd:hd + 1, :] = thr


def _peer_prep_call(x1, shift, scale, norm_w, w_q, keys, tp, seq):
    t, d = x1.shape
    per_seq = seq // tp
    full = lambda a: pl.BlockSpec(a.shape, lambda i: (0,) * a.ndim)
    per_b = pl.BlockSpec((1, 1, d), lambda i: (i // per_seq, 0, 0))
    hk = pl.BlockSpec((PEER_HEADS, PEER_KEYS, tp), lambda i: (0, 0, i))
    hk_shape = jax.ShapeDtypeStruct((PEER_HEADS, PEER_KEYS, t), F32)
    return pl.pallas_call(
        _peer_prep_kernel,
        grid=(t // tp,),
        in_specs=[pl.BlockSpec((tp, d), lambda i: (i, 0)), per_b, per_b, full(norm_w), full(w_q),
                  full(keys)],
        out_specs=(pl.BlockSpec((d, tp), lambda i: (0, i)), hk, hk, hk, hk,
                   pl.BlockSpec((PEER_HEADS, tp), lambda i: (0, i))),
        out_shape=(jax.ShapeDtypeStruct((d, t), BF16), hk_shape, hk_shape, hk_shape, hk_shape,
                   jax.ShapeDtypeStruct((PEER_HEADS, t), F32)),
        compiler_params=_cparams(1),
        name="peer_prep",
    )(x1, shift, scale, norm_w, w_q, keys)


def _peer_dense_kernel(ht_ref, u_ref, vt_ref, s1_ref, e1_ref, s2_ref, e2_ref, thr_ref, x_ref,
                       gate_ref, fnw_ref, o_ref, acc_ref, wg_ref):
    j = pl.program_id(1)

    @pl.when(j == 0)
    def _():
        acc_ref[...] = jnp.zeros_like(acc_ref)

    ht = ht_ref[...]
    n_i1 = s1_ref.shape[1]
    for i1 in range(n_i1):
        rs = slice(i1 * PEER_KEYS, (i1 + 1) * PEER_KEYS)
        act = jax.nn.gelu(_dot(u_ref[rs, :], ht))
        w = jnp.zeros_like(act)
        for hd in range(PEER_HEADS):
            sel = s1_ref[hd, i1:i1 + 1, :] + s2_ref[hd] >= thr_ref[hd:hd + 1, :]
            w = w + jnp.where(sel, e1_ref[hd, i1:i1 + 1, :] * e2_ref[hd], 0.0)
        wg_ref[rs, :] = (w * act).astype(BF16)
    acc_ref[...] += _dot(vt_ref[...], wg_ref[...])

    @pl.when(j == pl.num_programs(1) - 1)
    def _():
        x2 = x_ref[...] + gate_ref[0] * acc_ref[...].T
        o_ref[...] = _rms(x2, fnw_ref[...])


def _peer_dense_call(ht, u_b, vt_b, s1, e1, s2, e2, thr, x1, gate, fnw, tb, eb, seq):
    d, t = ht.shape
    e = u_b.shape[0]
    per_seq = seq // tb
    n_i1 = eb // PEER_KEYS
    rows1 = pl.BlockSpec((PEER_HEADS, n_i1, tb), lambda i, j: (0, j, i))
    rows2 = pl.BlockSpec((PEER_HEADS, PEER_KEYS, tb), lambda i, j: (0, 0, i))
    return pl.pallas_call(
        _peer_dense_kernel,
        grid=(t // tb, e // eb),
        in_specs=[pl.BlockSpec((d, tb), lambda i, j: (0, i)),
                  pl.BlockSpec((eb, d), lambda i, j: (j, 0)),
                  pl.BlockSpec((d, eb), lambda i, j: (0, j)),
                  rows1, rows1, rows2, rows2,
                  pl.BlockSpec((PEER_HEADS, tb), lambda i, j: (0, i)),
                  pl.BlockSpec((tb, d), lambda i, j: (i, 0)),
                  pl.BlockSpec((1, 1, d), lambda i, j: (i // per_seq, 0, 0)),
                  pl.BlockSpec((1, d), lambda i, j: (0, 0))],
        out_specs=pl.BlockSpec((tb, d), lambda i, j: (i, 0)),
        out_shape=jax.ShapeDtypeStruct((t, d), F32),
        scratch_shapes=[pltpu.VMEM((d, tb), F32), pltpu.VMEM((eb, tb), BF16)],
        compiler_params=_cparams(2),
        name="peer_dense",
    )(ht, u_b, vt_b, s1, e1, s2, e2, thr, x1, gate, fnw)


def _tile_sizes(seq):
    tm = min(512, seq)
    return tm, min(512, seq), min(512, seq), 2048


def kernel(x, c, ctx, c_ctx, w_mod, b_mod, norm1_w, w_in, conv_w, conv_b, dt_bias, a_log, d_skip,
           ssd_norm_w, gmlp_norm_w, gmlp_ws, gmlp_bs, w_out, norm2_w, peer_wq, peer_keys, peer_u,
           peer_v, final_norm_w):
    b, seq, d = x.shape
    assert w_mod.shape[0] == 1, "single-layer block"
    assert seq % CHUNK == 0 and ctx.shape[1] % CHUNK == 0 and b + 1 <= SUBLANES
    tm, tp, tb, eb = _tile_sizes(seq)

    cc = jnp.zeros((SUBLANES, d), F32).at[:b].set(c).at[b].set(c_ctx)
    mod = _mod_call(cc, w_mod[0], b_mod[0][None, :]).reshape(SUBLANES, 6, d)
    mod_x = [mod[:b, k][:, None, :] for k in range(6)]
    shift_s, scale_s = mod[b, 0][None, :], mod[b, 1][None, :]

    nf = d // 4
    omega = (1.0 / (10000.0 ** (jnp.arange(nf, dtype=F32) / nf)))[None, :]
    pos = _pos_call(seq, d, omega)

    w = w_in[0]
    o1, o2 = SSD_WIDTH, SSD_WIDTH + CONV_CH
    o3 = o2 + 2 * SSD_HEADS
    dt_pad = jnp.zeros((d, LANES - 2 * SSD_HEADS), F32)
    w_dt = jnp.concatenate([w[:, o2:o3], dt_pad], axis=1)
    w_main = jnp.concatenate([w[:, :o2], w[:, o3:], w_dt], axis=1).astype(BF16)
    w_xd = jnp.concatenate([w[:, o1:o2], w_dt], axis=1).astype(BF16)
    pad1 = jnp.zeros((1, LANES - 2 * SSD_HEADS), F32)
    dtb = jnp.concatenate([dt_bias[0].reshape(1, -1), pad1], axis=1)
    alog = jnp.concatenate([a_log[0].reshape(1, -1), pad1], axis=1)
    dskip = jnp.repeat(d_skip[0], SSD_HEAD_DIM)[None, :]
    cw, cb = conv_w[0], conv_b[0][None, :]
    n1 = norm1_w[0][None, :]

    st_f, st_b = _ctx_call(ctx, shift_s, scale_s, n1, w_xd, cw, cb, dtb, alog)

    gw = gmlp_ws[0].astype(BF16)
    gnw = gmlp_norm_w[0].reshape(1, GMLP_WIDTH)
    gb = jnp.repeat(gmlp_bs[0].T, GMLP_GROUP_DIM, axis=1)
    xp, z, xbc_raw, dt_raw, y_g = _inproj_call(x, pos, mod_x[0], mod_x[1], n1, w_main, gw, gnw, gb, tm)

    y_f = _ssd_fwd_call(xbc_raw, dt_raw, cw, cb, dtb, alog, dskip, st_f)
    x1 = _ssd_bwd_call(xbc_raw, dt_raw, cw, cb, dtb, alog, st_b, y_f, z, y_g, xp, mod_x[2],
                       ssd_norm_w[0][None, :], w_out[0].astype(BF16))

    x1f = x1.reshape(b * seq, d)
    ht, s1, e1, s2, e2, thr = _peer_prep_call(
        x1f, mod_x[3], mod_x[4], norm2_w[0][None, :], peer_wq[0].astype(BF16),
        peer_keys[0].astype(BF16), tp, seq)
    out = _peer_dense_call(ht, peer_u[0].astype(BF16), peer_v[0].T.astype(BF16), s1, e1, s2, e2, thr,
                           x1f, mod_x[5], final_norm_w[None, :], tb, eb, seq)
    return out.reshape(b, seq, d)
```

```python
import functools
import math

import jax
import jax.numpy as jnp
from jax import lax
from jax.experimental import pallas as pl
from jax.experimental.pallas import tpu as pltpu

F32 = jnp.float32
BF16 = jnp.bfloat16
HI = lax.Precision.HIGHEST

GRID_W = 64
SSD_HEAD_DIM = 64
SSD_HEADS = 16
SSD_GROUPS = 2
HEADS_PER_GROUP = SSD_HEADS // SSD_GROUPS
SSD_STATE = 128
CHUNK = 128
SSD_WIDTH = SSD_HEADS * SSD_HEAD_DIM
GN = SSD_GROUPS * SSD_STATE
CONV_CH = SSD_WIDTH + 2 * GN
GMLP_GROUPS = 8
GMLP_GROUP_DIM = 128
GMLP_WIDTH = GMLP_GROUPS * GMLP_GROUP_DIM
PEER_KEYS = 128
PEER_HEADS = 8
PEER_HALF_DIM = 128
PEER_TOPK = 16
EPS = 1e-6

LANES = 128
SUBLANES = 8
BF16_ROWS = 16
VMEM_LIMIT = 56 * 1024 * 1024

GROUP_W = HEADS_PER_GROUP * SSD_HEAD_DIM


def _cparams(n_axes):
    return pltpu.CompilerParams(
        dimension_semantics=("arbitrary",) * n_axes,
        vmem_limit_bytes=VMEM_LIMIT)


def _rms(x, w):
    return x * lax.rsqrt(jnp.mean(x * x, axis=-1, keepdims=True) + EPS) * w


def _silu(x):
    return x * jax.nn.sigmoid(x)


def _dot(a, b):
    return jnp.dot(a, b, preferred_element_type=F32)


def _dot_nt(a, b):
    return lax.dot_general(a, b, (((1,), (1,)), ((), ())), preferred_element_type=F32)


def _mod_kernel(c_ref, w_ref, b_ref, o_ref):
    a = _silu(c_ref[...]).astype(BF16)
    o_ref[...] = _dot(a, w_ref[...].astype(BF16)) + b_ref[...]


def _mod_call(cc, w_mod, b_mod):
    d, n = w_mod.shape
    bn = 1536
    return pl.pallas_call(
        _mod_kernel,
        grid=(n // bn,),
        in_specs=[pl.BlockSpec((SUBLANES, d), lambda j: (0, 0)),
                  pl.BlockSpec((d, bn), lambda j: (0, j)),
                  pl.BlockSpec((1, bn), lambda j: (0, j))],
        out_specs=pl.BlockSpec((SUBLANES, bn), lambda j: (0, j)),
        out_shape=jax.ShapeDtypeStruct((SUBLANES, n), F32),
        compiler_params=_cparams(1),
        name="mod",
    )(cc, w_mod, b_mod)


def _pos_kernel(omega_ref, o_ref):
    nf = omega_ref.shape[1]
    om = omega_ref[...]
    r = pl.program_id(0).astype(F32)
    col = lax.broadcasted_iota(jnp.int32, (GRID_W, nf), 0).astype(F32)
    ar = jnp.broadcast_to(r * om, (GRID_W, nf))
    ac = col * om
    o_ref[:, 0 * nf:1 * nf] = jnp.sin(ar)
    o_ref[:, 1 * nf:2 * nf] = jnp.cos(ar)
    o_ref[:, 2 * nf:3 * nf] = jnp.sin(ac)
    o_ref[:, 3 * nf:4 * nf] = jnp.cos(ac)


def _pos_call(seq, dim, omega):
    return pl.pallas_call(
        _pos_kernel,
        grid=(seq // GRID_W,),
        in_specs=[pl.BlockSpec((1, dim // 4), lambda i: (0, 0))],
        out_specs=pl.BlockSpec((GRID_W, dim), lambda i: (i, 0)),
        out_shape=jax.ShapeDtypeStruct((seq, dim), F32),
        compiler_params=_cparams(1),
        name="pos",
    )(omega)


def _conv_silu(x, prev_row, next_row, w, b):
    n = x.shape[0]
    rows = lax.broadcasted_iota(jnp.int32, x.shape, 0)
    x_prev = jnp.where(rows == 0, prev_row, pltpu.roll(x, 1, axis=0))
    x_next = jnp.where(rows == n - 1, next_row, pltpu.roll(x, n - 1, axis=0))
    y = b + x_prev * w[0:1, :] + x * w[1:2, :] + x_next * w[2:3, :]
    return _silu(y)


def _softplus(x):
    return jnp.maximum(x, 0.0) + jnp.log1p(jnp.exp(-jnp.abs(x)))


def _ssd_chunk(xbc, dt_all, a_all, expand, h_ref, reverse, want_y):
    q = xbc.shape[0]
    col0 = SSD_HEADS if reverse else 0
    xs = xbc[:, :SSD_WIDTH]
    d_a = dt_all * a_all
    ti = lax.broadcasted_iota(jnp.int32, (q, q), 0)
    si = lax.broadcasted_iota(jnp.int32, (q, q), 1)
    keep = (si >= ti) if reverse else (si <= ti)
    tri = jnp.where(keep, 1.0, 0.0).astype(F32)
    acs = jnp.dot(tri, d_a, precision=HI, preferred_element_type=F32)
    last = 0 if reverse else q - 1
    total = acs[last:last + 1, :]
    stacked = jnp.concatenate([jnp.exp(acs), dt_all, jnp.exp(total - acs)], axis=0)
    ex = jnp.dot(stacked, expand, precision=HI, preferred_element_type=F32)
    ea_x, dt_x, dte_x = ex[:q], ex[q:2 * q], ex[2 * q:]
    xdt = xs * dt_x
    xd = (xdt * dte_x).astype(BF16)
    xdt_b = xdt.astype(BF16)
    chunk_decay = ea_x[last:last + 1, :]
    acs_t = acs.T if want_y else None
    lane = lax.broadcasted_iota(jnp.int32, (q, LANES), 1)
    ys = []
    for g in range(SSD_GROUPS):
        bm = xbc[:, SSD_WIDTH + g * SSD_STATE:SSD_WIDTH + (g + 1) * SSD_STATE]
        cm = xbc[:, SSD_WIDTH + GN + g * SSD_STATE:SSD_WIDTH + GN + (g + 1) * SSD_STATE]
        gsl = slice(g * GROUP_W, (g + 1) * GROUP_W)
        h_old = h_ref[g]
        if want_y:
            cm_b = cm.astype(BF16)
            cb = _dot_nt(cm_b, bm.astype(BF16))
            y_off = _dot(cm_b, h_old.astype(BF16)) * ea_x[:, gsl]
            pieces = []
            for pr in range(HEADS_PER_GROUP // 2):
                ms = []
                for r in (2 * pr, 2 * pr + 1):
                    j = col0 + g * HEADS_PER_GROUP + r
                    seg = acs[:, j:j + 1] - acs_t[j:j + 1, :]
                    lm = jnp.exp(jnp.where(keep, seg, -jnp.inf))
                    ms.append((cb * lm).astype(BF16))
                c0 = g * GROUP_W + pr * LANES
                xp = xdt_b[:, c0:c0 + LANES]
                zero = jnp.zeros_like(xp)
                rhs = jnp.concatenate([jnp.where(lane < SSD_HEAD_DIM, xp, zero),
                                       jnp.where(lane >= SSD_HEAD_DIM, xp, zero)], axis=0)
                pieces.append(_dot(jnp.concatenate(ms, axis=1), rhs))
            ys.append(jnp.concatenate(pieces, axis=1) + y_off)
        h_ref[g] = h_old * chunk_decay[:, gsl] + _dot(bm.T.astype(BF16), xd[:, gsl])
    return jnp.concatenate(ys, axis=1) if want_y else None


def _expand_matrix(reverse):
    rows = lax.broadcasted_iota(jnp.int32, (LANES, SSD_WIDTH), 0)
    cols = lax.broadcasted_iota(jnp.int32, (LANES, SSD_WIDTH), 1)
    head = cols // SSD_HEAD_DIM + (SSD_HEADS if reverse else 0)
    return jnp.where(rows == head, 1.0, 0.0).astype(F32)


def _ctx_kernel(ctx_ref, shift_ref, scale_ref, nw_ref, w_ref, cw_ref, cb_ref, dtb_ref, alog_ref,
                stf_ref, stb_ref):
    h = _rms(ctx_ref[0], nw_ref[...]) * (1.0 + scale_ref[...]) + shift_ref[...]
    proj = _dot(h.astype(BF16), w_ref[...])
    n = proj.shape[0]
    zero_row = jnp.zeros((1, CONV_CH), F32)
    xbc = _conv_silu(proj[:, :CONV_CH], zero_row, zero_row, cw_ref[...], cb_ref[...])
    dt_all = _softplus(proj[:, CONV_CH:] + dtb_ref[...])
    a_all = -jnp.exp(alog_ref[...])
    stf_ref[...] = jnp.zeros_like(stf_ref)
    stb_ref[...] = jnp.zeros_like(stb_ref)
    nck = n // CHUNK
    ef, eb = _expand_matrix(False), _expand_matrix(True)
    for ci in range(nck):
        sl = slice(ci * CHUNK, (ci + 1) * CHUNK)
        _ssd_chunk(xbc[sl], dt_all[sl], a_all, ef, stf_ref.at[0], False, False)
    for ci in reversed(range(nck)):
        sl = slice(ci * CHUNK, (ci + 1) * CHUNK)
        _ssd_chunk(xbc[sl], dt_all[sl], a_all, eb, stb_ref.at[0], True, False)


def _ctx_call(ctx, shift, scale, norm_w, w_xd, conv_w, conv_b, dtb, alog):
    b, n, d = ctx.shape
    st_shape = jax.ShapeDtypeStruct((b, SSD_GROUPS, SSD_STATE, GROUP_W), F32)
    st_spec = pl.BlockSpec((1, SSD_GROUPS, SSD_STATE, GROUP_W), lambda i: (i, 0, 0, 0))
    full = lambda a: pl.BlockSpec(a.shape, lambda i: (0,) * a.ndim)
    return pl.pallas_call(
        _ctx_kernel,
        grid=(b,),
        in_specs=[pl.BlockSpec((1, n, d), lambda i: (i, 0, 0)),
                  full(shift), full(scale), full(norm_w), full(w_xd), full(conv_w), full(conv_b),
                  full(dtb), full(alog)],
        out_specs=(st_spec, st_spec),
        out_shape=(st_shape, st_shape),
        compiler_params=_cparams(1),
        name="ctx_state",
    )(ctx, shift, scale, norm_w, w_xd, conv_w, conv_b, dtb, alog)


def _inproj_kernel(x_ref, pos_ref, shift_ref, scale_ref, nw_ref, w_ref, gw_ref, gnw_ref, gb_ref,
                   xp_ref, z_ref, xbc_ref, dt_ref, yg_ref):
    xp = x_ref[0] + pos_ref[...]
    xp_ref[0] = xp
    h = _rms(xp, nw_ref[...]) * (1.0 + scale_ref[0]) + shift_ref[0]
    proj = _dot(h.astype(BF16), w_ref[...])
    z_ref[0] = proj[:, :SSD_WIDTH]
    xbc_ref[0] = proj[:, SSD_WIDTH:SSD_WIDTH + CONV_CH]
    o = SSD_WIDTH + CONV_CH
    u = jax.nn.gelu(proj[:, o:o + GMLP_WIDTH])
    v = jax.nn.gelu(proj[:, o + GMLP_WIDTH:o + 2 * GMLP_WIDTH])
    dt_ref[0] = proj[:, o + 2 * GMLP_WIDTH:]
    tm = u.shape[0]
    gnw = gnw_ref[...]
    gb = gb_ref[...]
    for g in range(GMLP_GROUPS):
        gs = slice(g * GMLP_GROUP_DIM, (g + 1) * GMLP_GROUP_DIM)
        vn = _rms(v[:, gs], gnw[:, gs]).astype(BF16)
        wg = gw_ref[g]
        for ci in range(tm // CHUNK):
            rs = slice(ci * CHUNK, (ci + 1) * CHUNK)
            mixed = _dot(wg, vn[rs]) + gb[:, gs]
            yg_ref[0, rs, gs] = (u[rs, gs] * mixed).astype(BF16)


def _inproj_call(x, pos, shift, scale, norm_w, w_main, gw, gnw, gb, tm):
    b, seq, d = x.shape
    nt = seq // tm
    full = lambda a: pl.BlockSpec(a.shape, lambda i, j: (0,) * a.ndim)
    tok = lambda w: pl.BlockSpec((1, tm, w), lambda i, j: (i, j, 0))
    per_b = pl.BlockSpec((1, 1, d), lambda i, j: (i, 0, 0))
    shp = lambda w, dt: jax.ShapeDtypeStruct((b, seq, w), dt)
    return pl.pallas_call(
        _inproj_kernel,
        grid=(b, nt),
        in_specs=[tok(d), pl.BlockSpec((tm, d), lambda i, j: (j, 0)), per_b, per_b, full(norm_w),
                  full(w_main), full(gw), full(gnw), full(gb)],
        out_specs=(tok(d), tok(SSD_WIDTH), tok(CONV_CH), tok(LANES), tok(GMLP_WIDTH)),
        out_shape=(shp(d, F32), shp(SSD_WIDTH, F32), shp(CONV_CH, F32), shp(LANES, F32),
                   shp(GMLP_WIDTH, BF16)),
        compiler_params=_cparams(2),
        name="inproj",
    )(x, pos, shift, scale, norm_w, w_main, gw, gnw, gb)


def _halo_rows(prev_ref, next_ref, c, nc):
    prev_row = jnp.where(c > 0, prev_ref[0, SUBLANES - 1:SUBLANES, :], 0.0)
    next_row = jnp.where(c < nc - 1, next_ref[0, 0:1, :], 0.0)
    return prev_row, next_row


def _ssd_fwd_kernel(xbc_ref, prev_ref, next_ref, dt_ref, cw_ref, cb_ref, dtb_ref, alog_ref,
                    dskip_ref, h0_ref, y_ref, h_ref):
    c = pl.program_id(1)
    nc = pl.num_programs(1)

    @pl.when(c == 0)
    def _():
        h_ref[...] = h0_ref[0]

    prev_row, next_row = _halo_rows(prev_ref, next_ref, c, nc)
    xbc = _conv_silu(xbc_ref[0], prev_row, next_row, cw_ref[...], cb_ref[...])
    dt_all = _softplus(dt_ref[0] + dtb_ref[...])
    a_all = -jnp.exp(alog_ref[...])
    y = _ssd_chunk(xbc, dt_all, a_all, _expand_matrix(False), h_ref, False, True)
    y_ref[0] = y + dskip_ref[...] * xbc[:, :SSD_WIDTH]


def _ssd_bwd_kernel(xbc_ref, prev_ref, next_ref, dt_ref, cw_ref, cb_ref, dtb_ref, alog_ref,
                    h0_ref, yf_ref, z_ref, yg_ref, xp_ref, gate_ref, snw_ref, wo_ref,
                    o_ref, h_ref):
    cr = pl.program_id(1)
    nc = pl.num_programs(1)
    c = nc - 1 - cr

    @pl.when(cr == 0)
    def _():
        h_ref[...] = h0_ref[0]

    prev_row, next_row = _halo_rows(prev_ref, next_ref, c, nc)
    xbc = _conv_silu(xbc_ref[0], prev_row, next_row, cw_ref[...], cb_ref[...])
    dt_all = _softplus(dt_ref[0] + dtb_ref[...])
    a_all = -jnp.exp(alog_ref[...])
    y = yf_ref[0] + _ssd_chunk(xbc, dt_all, a_all, _expand_matrix(True), h_ref, True, True)
    y_ssd = _rms(y * _silu(z_ref[0]), snw_ref[...])
    out = (_dot(y_ssd.astype(BF16), wo_ref[:SSD_WIDTH, :])
           + _dot(yg_ref[0], wo_ref[SSD_WIDTH:, :]))
    o_ref[0] = xp_ref[0] + gate_ref[0] * out


def _ssd_specs(seq, reverse):
    nc = seq // CHUNK
    hb = CHUNK // SUBLANES
    last_hb = seq // SUBLANES - 1
    cidx = (lambda j: nc - 1 - j) if reverse else (lambda j: j)
    cur = lambda w: pl.BlockSpec((1, CHUNK, w), lambda i, j: (i, cidx(j), 0))
    prev = pl.BlockSpec((1, SUBLANES, CONV_CH),
                        lambda i, j: (i, jnp.maximum(cidx(j) * hb - 1, 0), 0))
    nxt = pl.BlockSpec((1, SUBLANES, CONV_CH),
                       lambda i, j: (i, jnp.minimum((cidx(j) + 1) * hb, last_hb), 0))
    return nc, cur, prev, nxt


def _ssd_fwd_call(xbc_raw, dt_raw, conv_w, conv_b, dtb, alog, dskip, st_f):
    b, seq, _ = xbc_raw.shape
    nc, cur, prev, nxt = _ssd_specs(seq, False)
    full = lambda a: pl.BlockSpec(a.shape, lambda i, j: (0,) * a.ndim)
    st = pl.BlockSpec((1, SSD_GROUPS, SSD_STATE, GROUP_W), lambda i, j: (i, 0, 0, 0))
    return pl.pallas_call(
        _ssd_fwd_kernel,
        grid=(b, nc),
        in_specs=[cur(CONV_CH), prev, nxt, cur(LANES), full(conv_w), full(conv_b), full(dtb),
                  full(alog), full(dskip), st],
        out_specs=cur(SSD_WIDTH),
        out_shape=jax.ShapeDtypeStruct((b, seq, SSD_WIDTH), F32),
        scratch_shapes=[pltpu.VMEM((SSD_GROUPS, SSD_STATE, GROUP_W), F32)],
        compiler_params=_cparams(2),
        name="ssd_fwd",
    )(xbc_raw, xbc_raw, xbc_raw, dt_raw, conv_w, conv_b, dtb, alog, dskip, st_f)


def _ssd_bwd_call(xbc_raw, dt_raw, conv_w, conv_b, dtb, alog, st_b, y_f, z, y_g, xp, gate, snw, w_out):
    b, seq, d = xp.shape
    nc, cur, prev, nxt = _ssd_specs(seq, True)
    full = lambda a: pl.BlockSpec(a.shape, lambda i, j: (0,) * a.ndim)
    st = pl.BlockSpec((1, SSD_GROUPS, SSD_STATE, GROUP_W), lambda i, j: (i, 0, 0, 0))
    per_b = pl.BlockSpec((1, 1, d), lambda i, j: (i, 0, 0))
    return pl.pallas_call(
        _ssd_bwd_kernel,
        grid=(b, nc),
        in_specs=[cur(CONV_CH), prev, nxt, cur(LANES), full(conv_w), full(conv_b), full(dtb),
                  full(alog), st, cur(SSD_WIDTH), cur(SSD_WIDTH), cur(GMLP_WIDTH), cur(d), per_b,
                  full(snw), full(w_out)],
        out_specs=cur(d),
        out_shape=jax.ShapeDtypeStruct((b, seq, d), F32),
        scratch_shapes=[pltpu.VMEM((SSD_GROUPS, SSD_STATE, GROUP_W), F32)],
        compiler_params=_cparams(2),
        name="ssd_bwd",
    )(xbc_raw, xbc_raw, xbc_raw, dt_raw, conv_w, conv_b, dtb, alog, st_b, y_f, z, y_g, xp, gate,
      snw, w_out)


def _top16_desc(s, want_code=False):
    out = []
    code = jnp.zeros_like(s)
    for r in range(PEER_TOPK):
        m = jnp.max(s, axis=0, keepdims=True)
        out.append(m)
        hit = s == m
        if want_code:
            code = jnp.where(hit, float(PEER_TOPK - r), code)
        s = jnp.where(hit, -jnp.inf, s)
    return (out, code) if want_code else out


def _peer_prep_kernel(x_ref, shift_ref, scale_ref, nw_ref, wq_ref, keys_ref,
                      ht_ref, c1_ref, e1_ref, code2_ref, e2_ref):
    h = _rms(x_ref[...], nw_ref[...]) * (1.0 + scale_ref[0]) + shift_ref[0]
    ht_ref[...] = h.T.astype(BF16)
    q = _dot(h.astype(BF16), wq_ref[...])
    tp = q.shape[0]
    neg = jnp.full((SUBLANES, tp), -jnp.inf, F32)
    row8 = lax.broadcasted_iota(jnp.int32, (SUBLANES, tp), 0)
    for hd in range(PEER_HEADS):
        s = []
        for k in range(2):
            o = (hd * 2 + k) * PEER_HALF_DIM
            s.append(_dot_nt(keys_ref[k], q[:, o:o + PEER_HALF_DIM].astype(BF16)))
        a = _top16_desc(s[0])
        b, code2 = _top16_desc(s[1], want_code=True)
        b16 = jnp.concatenate(b, axis=0)
        b8 = b16[:SUBLANES]
        cands = [a[0] + b16, a[1] + b8]
        for i in range(2, 8):
            cands.append(jnp.where(row8 < PEER_TOPK // (i + 1), a[i] + b8, neg))
        cands.append(jnp.concatenate(a[8:], axis=0) + b[0])
        cand = jnp.concatenate(cands, axis=0)
        thr = _top16_desc(cand)[-1]
        m = a[0] + b[0]
        zsum = jnp.sum(jnp.where(cand >= thr, jnp.exp(cand - m), 0.0), axis=0, keepdims=True)
        c1 = jnp.full_like(s[0], float(PEER_TOPK + 1))
        for r in reversed(range(PEER_TOPK)):
            count = jnp.sum(jnp.where(a[r] + b16 >= thr, 1.0, 0.0), axis=0, keepdims=True)
            c1 = jnp.where(s[0] >= a[r], float(PEER_TOPK + 1) - count, c1)
        c1_ref[hd] = c1
        e1_ref[hd] = jnp.exp(s[0] - a[0]) * (0.5 / zsum)
        code2_ref[hd] = code2.astype(BF16)
        e2_ref[hd] = jnp.exp(s[1] - b[0]).astype(BF16)


def _peer_prep_call(x1, shift, scale, norm_w, w_q, keys, tp, seq):
    t, d = x1.shape
    per_seq = seq // tp
    full = lambda a: pl.BlockSpec(a.shape, lambda i: (0,) * a.ndim)
    per_b = pl.BlockSpec((1, 1, d), lambda i: (i // per_seq, 0, 0))
    hk = pl.BlockSpec((PEER_HEADS, PEER_KEYS, tp), lambda i: (0, 0, i))
    hk_shape = lambda dt: jax.ShapeDtypeStruct((PEER_HEADS, PEER_KEYS, t), dt)
    return pl.pallas_call(
        _peer_prep_kernel,
        grid=(t // tp,),
        in_specs=[pl.BlockSpec((tp, d), lambda i: (i, 0)), per_b, per_b, full(norm_w), full(w_q),
                  full(keys)],
        out_specs=(pl.BlockSpec((d, tp), lambda i: (0, i)), hk, hk, hk, hk),
        out_shape=(jax.ShapeDtypeStruct((d, t), BF16), hk_shape(F32), hk_shape(F32),
                   hk_shape(BF16), hk_shape(BF16)),
        compiler_params=_cparams(1),
        name="peer_prep",
    )(x1, shift, scale, norm_w, w_q, keys)


def _gelu_x2(x):
    k = math.sqrt(2.0 / math.pi)
    inner = x * (x * x * (0.044715 * k) + k)
    return x * jnp.tanh(inner) + x


def _peer_dense_kernel(n_eb, ht_ref, u_ref, vt_ref, c1_ref, e1_ref, code2_ref, e2_ref, x_ref,
                       gate_ref, fnw_ref, o_ref, acc_ref, act_a_ref, act_b_ref, wg_ref):
    s = pl.program_id(0)
    jv = lax.rem(jnp.maximum(s - 1, 0), n_eb)
    slot = lax.rem(s, 2)

    @pl.when(s == 0)
    def _():
        act_b_ref[...] = jnp.zeros_like(act_b_ref)

    @pl.when(jv == 0)
    def _():
        acc_ref[...] = jnp.zeros_like(acc_ref)

    tb = act_a_ref.shape[1]
    n_tiles = PEER_KEYS // BF16_ROWS

    def step(act_new, act_prev):
        act_new[...] = _dot(u_ref[...], ht_ref[...])
        for i1 in range(c1_ref.shape[1]):
            w = [jnp.zeros((BF16_ROWS, tb), BF16) for _ in range(n_tiles)]
            for hd in range(PEER_HEADS):
                c_row = jnp.broadcast_to(c1_ref[hd, i1:i1 + 1, :], (BF16_ROWS, tb)).astype(BF16)
                e_row = jnp.broadcast_to(e1_ref[hd, i1:i1 + 1, :], (BF16_ROWS, tb)).astype(BF16)
                for it in range(n_tiles):
                    rs = slice(it * BF16_ROWS, (it + 1) * BF16_ROWS)
                    sel = code2_ref[hd, rs, :] >= c_row
                    w[it] = w[it] + jnp.where(sel, e2_ref[hd, rs, :] * e_row,
                                              jnp.zeros_like(e_row))
            rows = slice(i1 * PEER_KEYS, (i1 + 1) * PEER_KEYS)
            wg_ref[rows, :] = jnp.concatenate(w, axis=0) * _gelu_x2(act_prev[rows, :]).astype(BF16)
        acc_ref[...] += _dot(vt_ref[...], wg_ref[...])

    pl.when(slot == 0)(lambda: step(act_a_ref, act_b_ref))
    pl.when(slot == 1)(lambda: step(act_b_ref, act_a_ref))

    @pl.when(jnp.logical_and(s > 0, jv == n_eb - 1))
    def _():
        x2 = x_ref[...] + gate_ref[0] * acc_ref[...].T
        o_ref[...] = _rms(x2, fnw_ref[...])


def _peer_dense_call(ht, u_b, vt_b, c1, e1, code2, e2, x1, gate, fnw, tb, eb, seq):
    d, t = ht.shape
    e = u_b.shape[0]
    per_seq = seq // tb
    n_i1 = eb // PEER_KEYS
    n_eb = e // eb
    total = (t // tb) * n_eb
    ti1 = lambda s: jnp.minimum(s, total - 1) // n_eb
    ej1 = lambda s: jnp.minimum(s, total - 1) % n_eb
    ti2 = lambda s: jnp.maximum(s - 1, 0) // n_eb
    ej2 = lambda s: jnp.maximum(s - 1, 0) % n_eb
    rows1 = pl.BlockSpec((PEER_HEADS, n_i1, tb), lambda s: (0, ej2(s), ti2(s)))
    rows2 = pl.BlockSpec((PEER_HEADS, PEER_KEYS, tb), lambda s: (0, 0, ti2(s)))
    return pl.pallas_call(
        functools.partial(_peer_dense_kernel, n_eb),
        grid=(total + 1,),
        in_specs=[pl.BlockSpec((d, tb), lambda s: (0, ti1(s))),
                  pl.BlockSpec((eb, d), lambda s: (ej1(s), 0)),
                  pl.BlockSpec((d, eb), lambda s: (0, ej2(s))),
                  rows1, rows1, rows2, rows2,
                  pl.BlockSpec((tb, d), lambda s: (ti2(s), 0)),
                  pl.BlockSpec((1, 1, d), lambda s: (ti2(s) // per_seq, 0, 0)),
                  pl.BlockSpec((1, d), lambda s: (0, 0))],
        out_specs=pl.BlockSpec((tb, d), lambda s: (ti2(s), 0)),
        out_shape=jax.ShapeDtypeStruct((t, d), F32),
        scratch_shapes=[pltpu.VMEM((d, tb), F32), pltpu.VMEM((eb, tb), F32),
                        pltpu.VMEM((eb, tb), F32), pltpu.VMEM((eb, tb), BF16)],
        compiler_params=_cparams(1),
        name="peer_dense",
    )(ht, u_b, vt_b, c1, e1, code2, e2, x1, gate, fnw)


def _tile_sizes(seq):
    tm = min(512, seq)
    return tm, min(512, seq), min(512, seq), 2048


def kernel(x, c, ctx, c_ctx, w_mod, b_mod, norm1_w, w_in, conv_w, conv_b, dt_bias, a_log, d_skip,
           ssd_norm_w, gmlp_norm_w, gmlp_ws, gmlp_bs, w_out, norm2_w, peer_wq, peer_keys, peer_u,
           peer_v, final_norm_w):
    b, seq, d = x.shape
    assert w_mod.shape[0] == 1, "single-layer block"
    assert seq % CHUNK == 0 and ctx.shape[1] % CHUNK == 0 and b + 1 <= SUBLANES
    tm, tp, tb, eb = _tile_sizes(seq)

    cc = jnp.zeros((SUBLANES, d), F32).at[:b].set(c).at[b].set(c_ctx)
    mod = _mod_call(cc, w_mod[0], b_mod[0][None, :]).reshape(SUBLANES, 6, d)
    mod_x = [mod[:b, k][:, None, :] for k in range(6)]
    shift_s, scale_s = mod[b, 0][None, :], mod[b, 1][None, :]

    nf = d // 4
    omega = (1.0 / (10000.0 ** (jnp.arange(nf, dtype=F32) / nf)))[None, :]
    pos = _pos_call(seq, d, omega)

    w = w_in[0]
    o1, o2 = SSD_WIDTH, SSD_WIDTH + CONV_CH
    o3 = o2 + 2 * SSD_HEADS
    dt_pad = jnp.zeros((d, LANES - 2 * SSD_HEADS), F32)
    w_dt = jnp.concatenate([w[:, o2:o3], dt_pad], axis=1)
    w_main = jnp.concatenate([w[:, :o2], w[:, o3:], w_dt], axis=1).astype(BF16)
    w_xd = jnp.concatenate([w[:, o1:o2], w_dt], axis=1).astype(BF16)
    pad1 = jnp.zeros((1, LANES - 2 * SSD_HEADS), F32)
    dtb = jnp.concatenate([dt_bias[0].reshape(1, -1), pad1], axis=1)
    alog = jnp.concatenate([a_log[0].reshape(1, -1), pad1], axis=1)
    dskip = jnp.repeat(d_skip[0], SSD_HEAD_DIM)[None, :]
    cw, cb = conv_w[0], conv_b[0][None, :]
    n1 = norm1_w[0][None, :]

    st_f, st_b = _ctx_call(ctx, shift_s, scale_s, n1, w_xd, cw, cb, dtb, alog)

    gw = gmlp_ws[0].astype(BF16)
    gnw = gmlp_norm_w[0].reshape(1, GMLP_WIDTH)
    gb = jnp.repeat(gmlp_bs[0].T, GMLP_GROUP_DIM, axis=1)
    xp, z, xbc_raw, dt_raw, y_g = _inproj_call(x, pos, mod_x[0], mod_x[1], n1, w_main, gw, gnw, gb, tm)

    y_f = _ssd_fwd_call(xbc_raw, dt_raw, cw, cb, dtb, alog, dskip, st_f)
    x1 = _ssd_bwd_call(xbc_raw, dt_raw, cw, cb, dtb, alog, st_b, y_f, z, y_g, xp, mod_x[2],
                       ssd_norm_w[0][None, :], w_out[0].astype(BF16))

    x1f = x1.reshape(b * seq, d)
    ht, c1, e1, code2, e2 = _peer_prep_call(
        x1f, mod_x[3], mod_x[4], norm2_w[0][None, :], peer_wq[0].astype(BF16),
        peer_keys[0].astype(BF16), tp, seq)
    out = _peer_dense_call(ht, peer_u[0].astype(BF16), peer_v[0].T.astype(BF16), c1, e1, code2, e2,
                           x1f, mod_x[5], final_norm_w[None, :], tb, eb, seq)
    return out.reshape(b, seq, d)
```

```python
import functools
import math

import jax
import jax.numpy as jnp
from jax import lax
from jax.experimental import pallas as pl
from jax.experimental.pallas import tpu as pltpu

F32 = jnp.float32
BF16 = jnp.bfloat16

GRID_W = 64
SSD_HEAD_DIM = 64
SSD_HEADS = 16
SSD_GROUPS = 2
HEADS_PER_GROUP = SSD_HEADS // SSD_GROUPS
SSD_STATE = 128
CHUNK = 128
SSD_WIDTH = SSD_HEADS * SSD_HEAD_DIM
GN = SSD_GROUPS * SSD_STATE
CONV_CH = SSD_WIDTH + 2 * GN
GMLP_GROUPS = 8
GMLP_GROUP_DIM = 128
GMLP_WIDTH = GMLP_GROUPS * GMLP_GROUP_DIM
PEER_KEYS = 128
PEER_HEADS = 8
PEER_HALF_DIM = 128
PEER_TOPK = 16
EPS = 1e-6

LANES = 128
SUBLANES = 8
BF16_ROWS = 16
VMEM_LIMIT = 56 * 1024 * 1024

GROUP_W = HEADS_PER_GROUP * SSD_HEAD_DIM
SSD_STEP_ROWS = 2 * CHUNK


def _cparams(n_axes, flags=None):
    return pltpu.CompilerParams(
        dimension_semantics=("arbitrary",) * n_axes,
        vmem_limit_bytes=VMEM_LIMIT, flags=flags)


def _rms(x, w):
    return x * lax.rsqrt(jnp.mean(x * x, axis=-1, keepdims=True) + EPS) * w


def _silu(x):
    return x * jax.nn.sigmoid(x)


def _dot(a, b):
    return jnp.dot(a, b, preferred_element_type=F32)


def _dot_nt(a, b):
    return lax.dot_general(a, b, (((1,), (1,)), ((), ())), preferred_element_type=F32)


def _mod_kernel(c_ref, w_ref, b_ref, o_ref):
    a = _silu(c_ref[...]).astype(BF16)
    o_ref[...] = _dot(a, w_ref[...].astype(BF16)) + b_ref[...]


def _mod_call(cc, w_mod, b_mod):
    d, n = w_mod.shape
    bn = 1536
    return pl.pallas_call(
        _mod_kernel,
        grid=(n // bn,),
        in_specs=[pl.BlockSpec((SUBLANES, d), lambda j: (0, 0)),
                  pl.BlockSpec((d, bn), lambda j: (0, j)),
                  pl.BlockSpec((1, bn), lambda j: (0, j))],
        out_specs=pl.BlockSpec((SUBLANES, bn), lambda j: (0, j)),
        out_shape=jax.ShapeDtypeStruct((SUBLANES, n), F32),
        compiler_params=_cparams(1),
        name="mod",
    )(cc, w_mod, b_mod)


def _pos_kernel(omega_ref, o_ref):
    nf = omega_ref.shape[1]
    om = omega_ref[...]
    r = pl.program_id(0).astype(F32)
    col = lax.broadcasted_iota(jnp.int32, (GRID_W, nf), 0).astype(F32)
    ar = jnp.broadcast_to(r * om, (GRID_W, nf))
    ac = col * om
    o_ref[:, 0 * nf:1 * nf] = jnp.sin(ar)
    o_ref[:, 1 * nf:2 * nf] = jnp.cos(ar)
    o_ref[:, 2 * nf:3 * nf] = jnp.sin(ac)
    o_ref[:, 3 * nf:4 * nf] = jnp.cos(ac)


def _pos_call(seq, dim, omega):
    return pl.pallas_call(
        _pos_kernel,
        grid=(seq // GRID_W,),
        in_specs=[pl.BlockSpec((1, dim // 4), lambda i: (0, 0))],
        out_specs=pl.BlockSpec((GRID_W, dim), lambda i: (i, 0)),
        out_shape=jax.ShapeDtypeStruct((seq, dim), F32),
        compiler_params=_cparams(1),
        name="pos",
    )(omega)


def _conv_silu(x, prev_row, next_row, w, b):
    n = x.shape[0]
    rows = lax.broadcasted_iota(jnp.int32, x.shape, 0)
    x_prev = jnp.where(rows == 0, prev_row, pltpu.roll(x, 1, axis=0))
    x_next = jnp.where(rows == n - 1, next_row, pltpu.roll(x, n - 1, axis=0))
    y = b + x_prev * w[0:1, :] + x * w[1:2, :] + x_next * w[2:3, :]
    return _silu(y)


def _softplus(x):
    return jnp.maximum(x, 0.0) + jnp.log1p(jnp.exp(-jnp.abs(x)))


def _split3(x):
    hi = x.astype(BF16)
    r = x - hi.astype(F32)
    mid = r.astype(BF16)
    return hi, mid, (r - mid.astype(F32)).astype(BF16)


def _dot_exact_rhs(x, m):
    return sum(_dot(p, m) for p in _split3(x))


def _dot_exact_lhs(m, x):
    return sum(_dot(m, p) for p in _split3(x))


def _ssd_chunk(xbc, dt_all, a_all, expand, h_ref, reverse, want_y):
    q = xbc.shape[0]
    col0 = SSD_HEADS if reverse else 0
    xs = xbc[:, :SSD_WIDTH]
    d_a = dt_all * a_all
    ti = lax.broadcasted_iota(jnp.int32, (q, q), 0)
    si = lax.broadcasted_iota(jnp.int32, (q, q), 1)
    keep = (si >= ti) if reverse else (si <= ti)
    tri = jnp.where(keep, 1.0, 0.0).astype(BF16)
    acs = _dot_exact_lhs(tri, d_a)
    last = 0 if reverse else q - 1
    total = acs[last:last + 1, :]
    stacked = jnp.concatenate([jnp.exp(acs), dt_all * jnp.exp(total - acs)], axis=0)
    ex = _dot_exact_rhs(stacked, expand)
    ea_x = ex[:q]
    xd = (xs * ex[q:]).astype(BF16)
    xs_b = xs.astype(BF16)
    chunk_decay = ea_x[last:last + 1, :]
    acs_t = acs.T if want_y else None
    dt_t = dt_all.T if want_y else None
    lane = lax.broadcasted_iota(jnp.int32, (q, LANES), 1)
    ys = []
    for g in range(SSD_GROUPS):
        bm = xbc[:, SSD_WIDTH + g * SSD_STATE:SSD_WIDTH + (g + 1) * SSD_STATE]
        cm = xbc[:, SSD_WIDTH + GN + g * SSD_STATE:SSD_WIDTH + GN + (g + 1) * SSD_STATE]
        gsl = slice(g * GROUP_W, (g + 1) * GROUP_W)
        h_old = h_ref[g]
        if want_y:
            cm_b = cm.astype(BF16)
            cb = _dot_nt(cm_b, bm.astype(BF16))
            y_off = _dot(cm_b, h_old.astype(BF16)) * ea_x[:, gsl]
            pieces = []
            for pr in range(HEADS_PER_GROUP // 2):
                ms = []
                for r in (2 * pr, 2 * pr + 1):
                    j = col0 + g * HEADS_PER_GROUP + r
                    seg = acs[:, j:j + 1] - acs_t[j:j + 1, :]
                    lm = jnp.exp(jnp.where(keep, seg, -jnp.inf))
                    ms.append((cb * lm * dt_t[j:j + 1, :]).astype(BF16))
                c0 = g * GROUP_W + pr * LANES
                xp = xs_b[:, c0:c0 + LANES]
                zero = jnp.zeros_like(xp)
                rhs = jnp.concatenate([jnp.where(lane < SSD_HEAD_DIM, xp, zero),
                                       jnp.where(lane >= SSD_HEAD_DIM, xp, zero)], axis=0)
                pieces.append(_dot(jnp.concatenate(ms, axis=1), rhs))
            ys.append(jnp.concatenate(pieces, axis=1) + y_off)
        h_ref[g] = h_old * chunk_decay[:, gsl] + _dot(bm.T.astype(BF16), xd[:, gsl])
    return jnp.concatenate(ys, axis=1) if want_y else None


def _expand_matrix(reverse):
    rows = lax.broadcasted_iota(jnp.int32, (LANES, SSD_WIDTH), 0)
    cols = lax.broadcasted_iota(jnp.int32, (LANES, SSD_WIDTH), 1)
    head = cols // SSD_HEAD_DIM + (SSD_HEADS if reverse else 0)
    return jnp.where(rows == head, 1.0, 0.0).astype(BF16)


def _ctx_kernel(ctx_ref, shift_ref, scale_ref, nw_ref, w_ref, cw_ref, cb_ref, dtb_ref, alog_ref,
                stf_ref, stb_ref):
    h = _rms(ctx_ref[0], nw_ref[...]) * (1.0 + scale_ref[...]) + shift_ref[...]
    proj = _dot(h.astype(BF16), w_ref[...])
    n = proj.shape[0]
    zero_row = jnp.zeros((1, CONV_CH), F32)
    xbc = _conv_silu(proj[:, :CONV_CH], zero_row, zero_row, cw_ref[...], cb_ref[...])
    dt_all = _softplus(proj[:, CONV_CH:] + dtb_ref[...])
    a_all = -jnp.exp(alog_ref[...])
    stf_ref[...] = jnp.zeros_like(stf_ref)
    stb_ref[...] = jnp.zeros_like(stb_ref)
    nck = n // CHUNK
    ef, eb = _expand_matrix(False), _expand_matrix(True)
    for ci in range(nck):
        sl = slice(ci * CHUNK, (ci + 1) * CHUNK)
        _ssd_chunk(xbc[sl], dt_all[sl], a_all, ef, stf_ref.at[0], False, False)
    for ci in reversed(range(nck)):
        sl = slice(ci * CHUNK, (ci + 1) * CHUNK)
        _ssd_chunk(xbc[sl], dt_all[sl], a_all, eb, stb_ref.at[0], True, False)


def _ctx_call(ctx, shift, scale, norm_w, w_xd, conv_w, conv_b, dtb, alog):
    b, n, d = ctx.shape
    st_shape = jax.ShapeDtypeStruct((b, SSD_GROUPS, SSD_STATE, GROUP_W), F32)
    st_spec = pl.BlockSpec((1, SSD_GROUPS, SSD_STATE, GROUP_W), lambda i: (i, 0, 0, 0))
    full = lambda a: pl.BlockSpec(a.shape, lambda i: (0,) * a.ndim)
    return pl.pallas_call(
        _ctx_kernel,
        grid=(b,),
        in_specs=[pl.BlockSpec((1, n, d), lambda i: (i, 0, 0)),
                  full(shift), full(scale), full(norm_w), full(w_xd), full(conv_w), full(conv_b),
                  full(dtb), full(alog)],
        out_specs=(st_spec, st_spec),
        out_shape=(st_shape, st_shape),
        compiler_params=_cparams(1),
        name="ctx_state",
    )(ctx, shift, scale, norm_w, w_xd, conv_w, conv_b, dtb, alog)


def _inproj_kernel(x_ref, pos_ref, shift_ref, scale_ref, nw_ref, w_ref, gw_ref, gnw_ref, gb_ref,
                   xp_ref, z_ref, xbc_ref, dt_ref, yg_ref):
    xp = x_ref[0] + pos_ref[...]
    xp_ref[0] = xp
    h = _rms(xp, nw_ref[...]) * (1.0 + scale_ref[0]) + shift_ref[0]
    proj = _dot(h.astype(BF16), w_ref[...])
    z_ref[0] = proj[:, :SSD_WIDTH]
    xbc_ref[0] = proj[:, SSD_WIDTH:SSD_WIDTH + CONV_CH]
    o = SSD_WIDTH + CONV_CH
    u = jax.nn.gelu(proj[:, o:o + GMLP_WIDTH])
    v = jax.nn.gelu(proj[:, o + GMLP_WIDTH:o + 2 * GMLP_WIDTH])
    dt_ref[0] = proj[:, o + 2 * GMLP_WIDTH:]
    tm = u.shape[0]
    gnw = gnw_ref[...]
    gb = gb_ref[...]
    for g in range(GMLP_GROUPS):
        gs = slice(g * GMLP_GROUP_DIM, (g + 1) * GMLP_GROUP_DIM)
        vn = _rms(v[:, gs], gnw[:, gs]).astype(BF16)
        wg = gw_ref[g]
        for ci in range(tm // CHUNK):
            rs = slice(ci * CHUNK, (ci + 1) * CHUNK)
            mixed = _dot(wg, vn[rs]) + gb[:, gs]
            yg_ref[0, rs, gs] = (u[rs, gs] * mixed).astype(BF16)


def _inproj_call(x, pos, shift, scale, norm_w, w_main, gw, gnw, gb, tm):
    b, seq, d = x.shape
    nt = seq // tm
    full = lambda a: pl.BlockSpec(a.shape, lambda i, j: (0,) * a.ndim)
    tok = lambda w: pl.BlockSpec((1, tm, w), lambda i, j: (i, j, 0))
    per_b = pl.BlockSpec((1, 1, d), lambda i, j: (i, 0, 0))
    shp = lambda w, dt: jax.ShapeDtypeStruct((b, seq, w), dt)
    return pl.pallas_call(
        _inproj_kernel,
        grid=(b, nt),
        in_specs=[tok(d), pl.BlockSpec((tm, d), lambda i, j: (j, 0)), per_b, per_b, full(norm_w),
                  full(w_main), full(gw), full(gnw), full(gb)],
        out_specs=(tok(d), tok(SSD_WIDTH), tok(CONV_CH), tok(LANES), tok(GMLP_WIDTH)),
        out_shape=(shp(d, F32), shp(SSD_WIDTH, F32), shp(CONV_CH, F32), shp(LANES, F32),
                   shp(GMLP_WIDTH, BF16)),
        compiler_params=_cparams(2),
        name="inproj",
    )(x, pos, shift, scale, norm_w, w_main, gw, gnw, gb)


def _halo_rows(prev_ref, next_ref, c, nc):
    prev_row = jnp.where(c > 0, prev_ref[0, SUBLANES - 1:SUBLANES, :], 0.0)
    next_row = jnp.where(c < nc - 1, next_ref[0, 0:1, :], 0.0)
    return prev_row, next_row


def _ssd_fwd_kernel(xbc_ref, prev_ref, next_ref, dt_ref, cw_ref, cb_ref, dtb_ref, alog_ref,
                    dskip_ref, h0_ref, y_ref, h_ref):
    c = pl.program_id(1)
    nc = pl.num_programs(1)

    @pl.when(c == 0)
    def _():
        h_ref[...] = h0_ref[0]

    prev_row, next_row = _halo_rows(prev_ref, next_ref, c, nc)
    xbc = _conv_silu(xbc_ref[0], prev_row, next_row, cw_ref[...], cb_ref[...])
    dt_all = _softplus(dt_ref[0] + dtb_ref[...])
    a_all = -jnp.exp(alog_ref[...])
    expand = _expand_matrix(False)
    for ci in range(xbc.shape[0] // CHUNK):
        rs = slice(ci * CHUNK, (ci + 1) * CHUNK)
        y = _ssd_chunk(xbc[rs], dt_all[rs], a_all, expand, h_ref, False, True)
        y_ref[0, rs, :] = y + dskip_ref[...] * xbc[rs, :SSD_WIDTH]


def _ssd_bwd_kernel(xbc_ref, prev_ref, next_ref, dt_ref, cw_ref, cb_ref, dtb_ref, alog_ref,
                    h0_ref, yf_ref, z_ref, yg_ref, xp_ref, gate_ref, snw_ref, wo_ref,
                    o_ref, h_ref):
    cr = pl.program_id(1)
    nc = pl.num_programs(1)
    c = nc - 1 - cr

    @pl.when(cr == 0)
    def _():
        h_ref[...] = h0_ref[0]

    prev_row, next_row = _halo_rows(prev_ref, next_ref, c, nc)
    xbc = _conv_silu(xbc_ref[0], prev_row, next_row, cw_ref[...], cb_ref[...])
    dt_all = _softplus(dt_ref[0] + dtb_ref[...])
    a_all = -jnp.exp(alog_ref[...])
    expand = _expand_matrix(True)
    for ci in reversed(range(xbc.shape[0] // CHUNK)):
        rs = slice(ci * CHUNK, (ci + 1) * CHUNK)
        y = yf_ref[0, rs, :] + _ssd_chunk(xbc[rs], dt_all[rs], a_all, expand, h_ref, True, True)
        y_ssd = _rms(y * _silu(z_ref[0, rs, :]), snw_ref[...])
        out = (_dot(y_ssd.astype(BF16), wo_ref[:SSD_WIDTH, :])
               + _dot(yg_ref[0, rs, :], wo_ref[SSD_WIDTH:, :]))
        o_ref[0, rs, :] = xp_ref[0, rs, :] + gate_ref[0] * out


def _ssd_specs(seq, reverse):
    rows = min(SSD_STEP_ROWS, seq)
    nc = seq // rows
    hb = rows // SUBLANES
    last_hb = seq // SUBLANES - 1
    cidx = (lambda j: nc - 1 - j) if reverse else (lambda j: j)
    cur = lambda w: pl.BlockSpec((1, rows, w), lambda i, j: (i, cidx(j), 0))
    prev = pl.BlockSpec((1, SUBLANES, CONV_CH),
                        lambda i, j: (i, jnp.maximum(cidx(j) * hb - 1, 0), 0))
    nxt = pl.BlockSpec((1, SUBLANES, CONV_CH),
                       lambda i, j: (i, jnp.minimum((cidx(j) + 1) * hb, last_hb), 0))
    return nc, cur, prev, nxt


def _ssd_fwd_call(xbc_raw, dt_raw, conv_w, conv_b, dtb, alog, dskip, st_f):
    b, seq, _ = xbc_raw.shape
    nc, cur, prev, nxt = _ssd_specs(seq, False)
    full = lambda a: pl.BlockSpec(a.shape, lambda i, j: (0,) * a.ndim)
    st = pl.BlockSpec((1, SSD_GROUPS, SSD_STATE, GROUP_W), lambda i, j: (i, 0, 0, 0))
    return pl.pallas_call(
        _ssd_fwd_kernel,
        grid=(b, nc),
        in_specs=[cur(CONV_CH), prev, nxt, cur(LANES), full(conv_w), full(conv_b), full(dtb),
                  full(alog), full(dskip), st],
        out_specs=cur(SSD_WIDTH),
        out_shape=jax.ShapeDtypeStruct((b, seq, SSD_WIDTH), F32),
        scratch_shapes=[pltpu.VMEM((SSD_GROUPS, SSD_STATE, GROUP_W), F32)],
        compiler_params=_cparams(2),
        name="ssd_fwd",
    )(xbc_raw, xbc_raw, xbc_raw, dt_raw, conv_w, conv_b, dtb, alog, dskip, st_f)


def _ssd_bwd_call(xbc_raw, dt_raw, conv_w, conv_b, dtb, alog, st_b, y_f, z, y_g, xp, gate, snw, w_out):
    b, seq, d = xp.shape
    nc, cur, prev, nxt = _ssd_specs(seq, True)
    full = lambda a: pl.BlockSpec(a.shape, lambda i, j: (0,) * a.ndim)
    st = pl.BlockSpec((1, SSD_GROUPS, SSD_STATE, GROUP_W), lambda i, j: (i, 0, 0, 0))
    per_b = pl.BlockSpec((1, 1, d), lambda i, j: (i, 0, 0))
    return pl.pallas_call(
        _ssd_bwd_kernel,
        grid=(b, nc),
        in_specs=[cur(CONV_CH), prev, nxt, cur(LANES), full(conv_w), full(conv_b), full(dtb),
                  full(alog), st, cur(SSD_WIDTH), cur(SSD_WIDTH), cur(GMLP_WIDTH), cur(d), per_b,
                  full(snw), full(w_out)],
        out_specs=cur(d),
        out_shape=jax.ShapeDtypeStruct((b, seq, d), F32),
        scratch_shapes=[pltpu.VMEM((SSD_GROUPS, SSD_STATE, GROUP_W), F32)],
        compiler_params=_cparams(2),
        name="ssd_bwd",
    )(xbc_raw, xbc_raw, xbc_raw, dt_raw, conv_w, conv_b, dtb, alog, st_b, y_f, z, y_g, xp, gate,
      snw, w_out)


def _oem_sort_pairs(n):
    pairs = []
    p = 1
    while p < n:
        k = p
        while k >= 1:
            for j in range(k % p, n - k, 2 * k):
                for i in range(min(k, n - j - k)):
                    if (i + j) // (2 * p) == (i + j + k) // (2 * p):
                        pairs.append((i + j, i + j + k))
            k //= 2
        p *= 2
    return pairs


def _cmp_exchange(x, y):
    if x is None:
        return y, None
    if y is None:
        return x, None
    return jnp.maximum(x, y), jnp.minimum(x, y)


def _merge_top(xs, ys, n):
    xs = list(xs) + [None] * (n - len(xs))
    ys = list(ys) + [None] * (n - len(ys))
    out = [_cmp_exchange(xs[k], ys[n - 1 - k])[0] for k in range(n)]
    stride = n // 2
    while stride >= 1:
        for i in range(n):
            if i & stride == 0:
                out[i], out[i + stride] = _cmp_exchange(out[i], out[i + stride])
        stride //= 2
    return [v for v in out if v is not None]


def _top16_sorted(vals):
    groups = []
    for g in range(0, len(vals), PEER_TOPK):
        grp = list(vals[g:g + PEER_TOPK])
        for i, j in _oem_sort_pairs(PEER_TOPK):
            grp[i], grp[j] = _cmp_exchange(grp[i], grp[j])
        groups.append(grp)
    while len(groups) > 1:
        groups = [_merge_top(groups[i], groups[i + 1], PEER_TOPK) for i in range(0, len(groups), 2)]
    return groups[0]


def _peer_prep_kernel(x_ref, shift_ref, scale_ref, nw_ref, wqt_ref, keys_ref,
                      ht_ref, c1_ref, e1_ref, code2_ref, e2_ref, s_scr, o_scr):
    h = _rms(x_ref[...], nw_ref[...]) * (1.0 + scale_ref[0]) + shift_ref[0]
    ht = h.T.astype(BF16)
    ht_ref[...] = ht
    tp = ht.shape[1]
    ntg = tp // LANES
    assert ntg == SUBLANES
    qdim = 2 * PEER_HALF_DIM

    def head(hd, carry):
        qt = _dot(wqt_ref[pl.ds(pl.multiple_of(hd * qdim, qdim), qdim), :], ht)
        vals = []
        for k in range(2):
            st = _dot(keys_ref[k], qt[k * PEER_HALF_DIM:(k + 1) * PEER_HALF_DIM].astype(BF16))
            for tg in range(ntg):
                s_scr[k, tg * PEER_KEYS:(tg + 1) * PEER_KEYS, :] = st[:, tg * LANES:(tg + 1) * LANES]
            vals.append([s_scr[k, pl.ds(key, ntg, stride=PEER_KEYS), :] for key in range(PEER_KEYS)])
        a = _top16_sorted(vals[0])
        b = _top16_sorted(vals[1])
        rows = [[a[i] + b[j] for j in range(PEER_TOPK // (i + 1))] for i in range(PEER_TOPK // 2)]
        col0 = [a[i] + b[0] for i in range(PEER_TOPK // 2, PEER_TOPK)]
        top = _merge_top(rows[0], _merge_top(rows[1], col0, PEER_TOPK), PEER_TOPK)
        rest = _merge_top(_merge_top(rows[2], rows[3], PEER_TOPK),
                          _merge_top(_merge_top(rows[4], rows[5], PEER_TOPK),
                                     _merge_top(rows[6], rows[7], PEER_TOPK), PEER_TOPK), PEER_TOPK)
        top = _merge_top(top, rest, PEER_TOPK)
        thr = top[-1]
        zsum = sum(jnp.exp(t - top[0]) for t in top[1:]) + 1.0
        half_inv_z = 0.5 / zsum
        none = float(PEER_TOPK + 1)
        c_rank = []
        for i in range(PEER_TOPK):
            row = rows[i] if i < PEER_TOPK // 2 else [col0[i - PEER_TOPK // 2]]
            c_rank.append(none - sum(jnp.where(v >= thr, 1.0, 0.0) for v in row))
        for kb in range(0, PEER_KEYS, SUBLANES):
            keys = range(kb, kb + SUBLANES)
            c1 = {key: jnp.full_like(thr, none) for key in keys}
            code2 = {key: jnp.zeros_like(thr) for key in keys}
            for r in reversed(range(PEER_TOPK)):
                for key in keys:
                    c1[key] = jnp.where(vals[0][key] >= a[r], c_rank[r], c1[key])
                    code2[key] = jnp.where(vals[1][key] >= b[r], float(PEER_TOPK - r), code2[key])
            for key in keys:
                sl = slice(key * ntg, (key + 1) * ntg)
                o_scr[0, sl, :] = c1[key]
                o_scr[1, sl, :] = jnp.exp(vals[0][key] - a[0]) * half_inv_z
                o_scr[2, sl, :] = code2[key]
                o_scr[3, sl, :] = jnp.exp(vals[1][key] - b[0])
        for tg in range(ntg):
            rs = pl.ds(tg, PEER_KEYS, stride=ntg)
            ls = slice(tg * LANES, (tg + 1) * LANES)
            c1_ref[hd, :, ls] = o_scr[0, rs, :]
            e1_ref[hd, :, ls] = o_scr[1, rs, :]
            code2_ref[hd, :, ls] = o_scr[2, rs, :].astype(BF16)
            e2_ref[hd, :, ls] = o_scr[3, rs, :].astype(BF16)
        return carry

    lax.fori_loop(0, PEER_HEADS, head, 0)


def _peer_prep_call(x1, shift, scale, norm_w, w_qt, keys, tp, seq):
    t, d = x1.shape
    per_seq = seq // tp
    full = lambda a: pl.BlockSpec(a.shape, lambda i: (0,) * a.ndim)
    per_b = pl.BlockSpec((1, 1, d), lambda i: (i // per_seq, 0, 0))
    hk = pl.BlockSpec((PEER_HEADS, PEER_KEYS, tp), lambda i: (0, 0, i))
    hk_shape = lambda dt: jax.ShapeDtypeStruct((PEER_HEADS, PEER_KEYS, t), dt)
    rows = (tp // LANES) * PEER_KEYS
    return pl.pallas_call(
        _peer_prep_kernel,
        grid=(t // tp,),
        in_specs=[pl.BlockSpec((tp, d), lambda i: (i, 0)), per_b, per_b, full(norm_w), full(w_qt),
                  full(keys)],
        out_specs=(pl.BlockSpec((d, tp), lambda i: (0, i)), hk, hk, hk, hk),
        out_shape=(jax.ShapeDtypeStruct((d, t), BF16), hk_shape(F32), hk_shape(F32),
                   hk_shape(BF16), hk_shape(BF16)),
        scratch_shapes=[pltpu.VMEM((2, rows, LANES), F32), pltpu.VMEM((4, rows, LANES), F32)],
        compiler_params=_cparams(1),
        name="peer_prep",
    )(x1, shift, scale, norm_w, w_qt, keys)


def _gelu_x2(x):
    k = math.sqrt(2.0 / math.pi)
    inner = x * (x * x * (0.044715 * k) + k)
    return x * jnp.tanh(inner) + x


def _peer_dense_kernel(n_eb, ht_ref, u_ref, vt_ref, c1_ref, e1_ref, code2_ref, e2_ref, x_ref,
                       gate_ref, fnw_ref, o_ref, acc_ref, act_a_ref, act_b_ref, wg_ref):
    s = pl.program_id(0)
    jv = lax.rem(jnp.maximum(s - 1, 0), n_eb)
    slot = lax.rem(s, 2)

    @pl.when(s == 0)
    def _():
        act_b_ref[...] = jnp.zeros_like(act_b_ref)

    @pl.when(jv == 0)
    def _():
        acc_ref[...] = jnp.zeros_like(acc_ref)

    tb = act_a_ref.shape[1]
    n_tiles = PEER_KEYS // BF16_ROWS

    def step(act_new, act_prev):
        act_new[...] = _dot(u_ref[...], ht_ref[...])
        for i1 in range(c1_ref.shape[1]):
            w = [jnp.zeros((BF16_ROWS, tb), BF16) for _ in range(n_tiles)]
            for hd in range(PEER_HEADS):
                c_row = jnp.broadcast_to(c1_ref[hd, i1:i1 + 1, :], (BF16_ROWS, tb)).astype(BF16)
                e_row = jnp.broadcast_to(e1_ref[hd, i1:i1 + 1, :], (BF16_ROWS, tb)).astype(BF16)
                for it in range(n_tiles):
                    rs = slice(it * BF16_ROWS, (it + 1) * BF16_ROWS)
                    sel = code2_ref[hd, rs, :] >= c_row
                    w[it] = w[it] + jnp.where(sel, e2_ref[hd, rs, :] * e_row,
                                              jnp.zeros_like(e_row))
            rows = slice(i1 * PEER_KEYS, (i1 + 1) * PEER_KEYS)
            wg_ref[rows, :] = jnp.concatenate(w, axis=0) * _gelu_x2(act_prev[rows, :]).astype(BF16)
        acc_ref[...] += _dot(vt_ref[...], wg_ref[...])

    pl.when(slot == 0)(lambda: step(act_a_ref, act_b_ref))
    pl.when(slot == 1)(lambda: step(act_b_ref, act_a_ref))

    @pl.when(jnp.logical_and(s > 0, jv == n_eb - 1))
    def _():
        x2 = x_ref[...] + gate_ref[0] * acc_ref[...].T
        o_ref[...] = _rms(x2, fnw_ref[...])


def _peer_dense_call(ht, u_b, vt_b, c1, e1, code2, e2, x1, gate, fnw, tb, eb, seq):
    d, t = ht.shape
    e = u_b.shape[0]
    per_seq = seq // tb
    n_i1 = eb // PEER_KEYS
    n_eb = e // eb
    total = (t // tb) * n_eb
    ti1 = lambda s: jnp.minimum(s, total - 1) // n_eb
    ej1 = lambda s: jnp.minimum(s, total - 1) % n_eb
    ti2 = lambda s: jnp.maximum(s - 1, 0) // n_eb
    ej2 = lambda s: jnp.maximum(s - 1, 0) % n_eb
    rows1 = pl.BlockSpec((PEER_HEADS, n_i1, tb), lambda s: (0, ej2(s), ti2(s)))
    rows2 = pl.BlockSpec((PEER_HEADS, PEER_KEYS, tb), lambda s: (0, 0, ti2(s)))
    return pl.pallas_call(
        functools.partial(_peer_dense_kernel, n_eb),
        grid=(total + 1,),
        in_specs=[pl.BlockSpec((d, tb), lambda s: (0, ti1(s))),
                  pl.BlockSpec((eb, d), lambda s: (ej1(s), 0)),
                  pl.BlockSpec((d, eb), lambda s: (0, ej2(s))),
                  rows1, rows1, rows2, rows2,
                  pl.BlockSpec((tb, d), lambda s: (ti2(s), 0)),
                  pl.BlockSpec((1, 1, d), lambda s: (ti2(s) // per_seq, 0, 0)),
                  pl.BlockSpec((1, d), lambda s: (0, 0))],
        out_specs=pl.BlockSpec((tb, d), lambda s: (ti2(s), 0)),
        out_shape=jax.ShapeDtypeStruct((t, d), F32),
        scratch_shapes=[pltpu.VMEM((d, tb), F32), pltpu.VMEM((eb, tb), F32),
                        pltpu.VMEM((eb, tb), F32), pltpu.VMEM((eb, tb), BF16)],
        compiler_params=_cparams(1),
        name="peer_dense",
    )(ht, u_b, vt_b, c1, e1, code2, e2, x1, gate, fnw)


def _tile_sizes(seq):
    tm = min(512, seq)
    return tm, SUBLANES * LANES, min(512, seq), 2048


def kernel(x, c, ctx, c_ctx, w_mod, b_mod, norm1_w, w_in, conv_w, conv_b, dt_bias, a_log, d_skip,
           ssd_norm_w, gmlp_norm_w, gmlp_ws, gmlp_bs, w_out, norm2_w, peer_wq, peer_keys, peer_u,
           peer_v, final_norm_w):
    b, seq, d = x.shape
    assert w_mod.shape[0] == 1, "single-layer block"
    assert seq % CHUNK == 0 and ctx.shape[1] % CHUNK == 0 and b + 1 <= SUBLANES
    tm, tp, tb, eb = _tile_sizes(seq)
    assert seq % tm == 0 and seq % tp == 0 and seq % tb == 0 and seq % SSD_STEP_ROWS == 0

    cc = jnp.zeros((SUBLANES, d), F32).at[:b].set(c).at[b].set(c_ctx)
    mod = _mod_call(cc, w_mod[0], b_mod[0][None, :]).reshape(SUBLANES, 6, d)
    mod_x = [mod[:b, k][:, None, :] for k in range(6)]
    shift_s, scale_s = mod[b, 0][None, :], mod[b, 1][None, :]

    nf = d // 4
    omega = (1.0 / (10000.0 ** (jnp.arange(nf, dtype=F32) / nf)))[None, :]
    pos = _pos_call(seq, d, omega)

    w = w_in[0]
    o1, o2 = SSD_WIDTH, SSD_WIDTH + CONV_CH
    o3 = o2 + 2 * SSD_HEADS
    dt_pad = jnp.zeros((d, LANES - 2 * SSD_HEADS), F32)
    w_dt = jnp.concatenate([w[:, o2:o3], dt_pad], axis=1)
    w_main = jnp.concatenate([w[:, :o2], w[:, o3:], w_dt], axis=1).astype(BF16)
    w_xd = jnp.concatenate([w[:, o1:o2], w_dt], axis=1).astype(BF16)
    pad1 = jnp.zeros((1, LANES - 2 * SSD_HEADS), F32)
    dtb = jnp.concatenate([dt_bias[0].reshape(1, -1), pad1], axis=1)
    alog = jnp.concatenate([a_log[0].reshape(1, -1), pad1], axis=1)
    dskip = jnp.repeat(d_skip[0], SSD_HEAD_DIM)[None, :]
    cw, cb = conv_w[0], conv_b[0][None, :]
    n1 = norm1_w[0][None, :]

    st_f, st_b = _ctx_call(ctx, shift_s, scale_s, n1, w_xd, cw, cb, dtb, alog)

    gw = gmlp_ws[0].astype(BF16)
    gnw = gmlp_norm_w[0].reshape(1, GMLP_WIDTH)
    gb = jnp.repeat(gmlp_bs[0].T, GMLP_GROUP_DIM, axis=1)
    xp, z, xbc_raw, dt_raw, y_g = _inproj_call(x, pos, mod_x[0], mod_x[1], n1, w_main, gw, gnw, gb, tm)

    y_f = _ssd_fwd_call(xbc_raw, dt_raw, cw, cb, dtb, alog, dskip, st_f)
    x1 = _ssd_bwd_call(xbc_raw, dt_raw, cw, cb, dtb, alog, st_b, y_f, z, y_g, xp, mod_x[2],
                       ssd_norm_w[0][None, :], w_out[0].astype(BF16))

    x1f = x1.reshape(b * seq, d)
    ht, c1, e1, code2, e2 = _peer_prep_call(
        x1f, mod_x[3], mod_x[4], norm2_w[0][None, :], peer_wq[0].T.astype(BF16),
        peer_keys[0].astype(BF16), tp, seq)
    out = _peer_dense_call(ht, peer_u[0].astype(BF16), peer_v[0].T.astype(BF16), c1, e1, code2, e2,
                           x1f, mod_x[5], final_norm_w[None, :], tb, eb, seq)
    return out.reshape(b, seq, d)
```

```python
import functools
import math

import jax
import jax.numpy as jnp
from jax import lax
from jax.experimental import pallas as pl
from jax.experimental.pallas import tpu as pltpu

F32 = jnp.float32
BF16 = jnp.bfloat16

GRID_W = 64
SSD_HEAD_DIM = 64
SSD_HEADS = 16
SSD_GROUPS = 2
HEADS_PER_GROUP = SSD_HEADS // SSD_GROUPS
SSD_STATE = 128
CHUNK = 128
SSD_WIDTH = SSD_HEADS * SSD_HEAD_DIM
GN = SSD_GROUPS * SSD_STATE
CONV_CH = SSD_WIDTH + 2 * GN
GMLP_GROUPS = 8
GMLP_GROUP_DIM = 128
GMLP_WIDTH = GMLP_GROUPS * GMLP_GROUP_DIM
PEER_KEYS = 128
PEER_HEADS = 8
PEER_HALF_DIM = 128
PEER_TOPK = 16
EPS = 1e-6

LANES = 128
SUBLANES = 8
BF16_ROWS = 16
VMEM_LIMIT = 56 * 1024 * 1024

GROUP_W = HEADS_PER_GROUP * SSD_HEAD_DIM
SSD_STEP_ROWS = 2 * CHUNK
GATE_LANE_GROUPS = 4


def _cparams(n_axes, flags=None):
    return pltpu.CompilerParams(
        dimension_semantics=("arbitrary",) * n_axes,
        vmem_limit_bytes=VMEM_LIMIT, flags=flags)


def _rms(x, w):
    return x * lax.rsqrt(jnp.mean(x * x, axis=-1, keepdims=True) + EPS) * w


def _silu(x):
    return x * jax.nn.sigmoid(x)


def _dot(a, b):
    return jnp.dot(a, b, preferred_element_type=F32)


def _dot_nt(a, b):
    return lax.dot_general(a, b, (((1,), (1,)), ((), ())), preferred_element_type=F32)


def _mod_kernel(c_ref, w_ref, b_ref, o_ref):
    a = _silu(c_ref[...]).astype(BF16)
    o_ref[...] = _dot(a, w_ref[...].astype(BF16)) + b_ref[...]


def _mod_call(cc, w_mod, b_mod):
    d, n = w_mod.shape
    bn = 1536
    return pl.pallas_call(
        _mod_kernel,
        grid=(n // bn,),
        in_specs=[pl.BlockSpec((SUBLANES, d), lambda j: (0, 0)),
                  pl.BlockSpec((d, bn), lambda j: (0, j)),
                  pl.BlockSpec((1, bn), lambda j: (0, j))],
        out_specs=pl.BlockSpec((SUBLANES, bn), lambda j: (0, j)),
        out_shape=jax.ShapeDtypeStruct((SUBLANES, n), F32),
        compiler_params=_cparams(1),
        name="mod",
    )(cc, w_mod, b_mod)


def _pos_kernel(omega_ref, o_ref):
    nf = omega_ref.shape[1]
    om = omega_ref[...]
    r = pl.program_id(0).astype(F32)
    col = lax.broadcasted_iota(jnp.int32, (GRID_W, nf), 0).astype(F32)
    ar = jnp.broadcast_to(r * om, (GRID_W, nf))
    ac = col * om
    o_ref[:, 0 * nf:1 * nf] = jnp.sin(ar)
    o_ref[:, 1 * nf:2 * nf] = jnp.cos(ar)
    o_ref[:, 2 * nf:3 * nf] = jnp.sin(ac)
    o_ref[:, 3 * nf:4 * nf] = jnp.cos(ac)


def _pos_call(seq, dim, omega):
    return pl.pallas_call(
        _pos_kernel,
        grid=(seq // GRID_W,),
        in_specs=[pl.BlockSpec((1, dim // 4), lambda i: (0, 0))],
        out_specs=pl.BlockSpec((GRID_W, dim), lambda i: (i, 0)),
        out_shape=jax.ShapeDtypeStruct((seq, dim), F32),
        compiler_params=_cparams(1),
        name="pos",
    )(omega)


def _conv_silu(x, prev_row, next_row, w, b):
    n = x.shape[0]
    rows = lax.broadcasted_iota(jnp.int32, x.shape, 0)
    x_prev = jnp.where(rows == 0, prev_row, pltpu.roll(x, 1, axis=0))
    x_next = jnp.where(rows == n - 1, next_row, pltpu.roll(x, n - 1, axis=0))
    y = b + x_prev * w[0:1, :] + x * w[1:2, :] + x_next * w[2:3, :]
    return _silu(y)


def _softplus(x):
    return jnp.maximum(x, 0.0) + jnp.log1p(jnp.exp(-jnp.abs(x)))


def _split3(x):
    hi = x.astype(BF16)
    r = x - hi.astype(F32)
    mid = r.astype(BF16)
    return hi, mid, (r - mid.astype(F32)).astype(BF16)


def _dot_exact_rhs(x, m):
    return sum(_dot(p, m) for p in _split3(x))


def _dot_exact_lhs(m, x):
    return sum(_dot(m, p) for p in _split3(x))


def _ssd_chunk(xbc, dt_all, a_all, expand, h_ref, reverse, want_y):
    q = xbc.shape[0]
    col0 = SSD_HEADS if reverse else 0
    xs = xbc[:, :SSD_WIDTH]
    d_a = dt_all * a_all
    ti = lax.broadcasted_iota(jnp.int32, (q, q), 0)
    si = lax.broadcasted_iota(jnp.int32, (q, q), 1)
    keep = (si >= ti) if reverse else (si <= ti)
    tri = jnp.where(keep, 1.0, 0.0).astype(BF16)
    acs = _dot_exact_lhs(tri, d_a)
    last = 0 if reverse else q - 1
    total = acs[last:last + 1, :]
    stacked = jnp.concatenate([jnp.exp(acs), dt_all * jnp.exp(total - acs)], axis=0)
    ex = _dot_exact_rhs(stacked, expand)
    ea_x = ex[:q]
    xd = (xs * ex[q:]).astype(BF16)
    xs_b = xs.astype(BF16)
    chunk_decay = ea_x[last:last + 1, :]
    acs_t = acs.T if want_y else None
    dt_t = dt_all.T if want_y else None
    lane = lax.broadcasted_iota(jnp.int32, (q, LANES), 1)
    ys = []
    for g in range(SSD_GROUPS):
        bm = xbc[:, SSD_WIDTH + g * SSD_STATE:SSD_WIDTH + (g + 1) * SSD_STATE]
        cm = xbc[:, SSD_WIDTH + GN + g * SSD_STATE:SSD_WIDTH + GN + (g + 1) * SSD_STATE]
        gsl = slice(g * GROUP_W, (g + 1) * GROUP_W)
        h_old = h_ref[g]
        if want_y:
            cm_b = cm.astype(BF16)
            cb = _dot_nt(cm_b, bm.astype(BF16))
            y_off = _dot(cm_b, h_old.astype(BF16)) * ea_x[:, gsl]
            pieces = []
            for pr in range(HEADS_PER_GROUP // 2):
                ms = []
                for r in (2 * pr, 2 * pr + 1):
                    j = col0 + g * HEADS_PER_GROUP + r
                    seg = acs[:, j:j + 1] - acs_t[j:j + 1, :]
                    lm = jnp.exp(jnp.where(keep, seg, -jnp.inf))
                    ms.append((cb * lm * dt_t[j:j + 1, :]).astype(BF16))
                c0 = g * GROUP_W + pr * LANES
                xp = xs_b[:, c0:c0 + LANES]
                zero = jnp.zeros_like(xp)
                rhs = jnp.concatenate([jnp.where(lane < SSD_HEAD_DIM, xp, zero),
                                       jnp.where(lane >= SSD_HEAD_DIM, xp, zero)], axis=0)
                pieces.append(_dot(jnp.concatenate(ms, axis=1), rhs))
            ys.append(jnp.concatenate(pieces, axis=1) + y_off)
        h_ref[g] = h_old * chunk_decay[:, gsl] + _dot(bm.T.astype(BF16), xd[:, gsl])
    return jnp.concatenate(ys, axis=1) if want_y else None


def _expand_matrix(reverse):
    rows = lax.broadcasted_iota(jnp.int32, (LANES, SSD_WIDTH), 0)
    cols = lax.broadcasted_iota(jnp.int32, (LANES, SSD_WIDTH), 1)
    head = cols // SSD_HEAD_DIM + (SSD_HEADS if reverse else 0)
    return jnp.where(rows == head, 1.0, 0.0).astype(BF16)


def _ctx_kernel(ctx_ref, shift_ref, scale_ref, nw_ref, w_ref, cw_ref, cb_ref, dtb_ref, alog_ref,
                stf_ref, stb_ref):
    h = _rms(ctx_ref[0], nw_ref[...]) * (1.0 + scale_ref[...]) + shift_ref[...]
    proj = _dot(h.astype(BF16), w_ref[...])
    n = proj.shape[0]
    zero_row = jnp.zeros((1, CONV_CH), F32)
    xbc = _conv_silu(proj[:, :CONV_CH], zero_row, zero_row, cw_ref[...], cb_ref[...])
    dt_all = _softplus(proj[:, CONV_CH:] + dtb_ref[...])
    a_all = -jnp.exp(alog_ref[...])
    stf_ref[...] = jnp.zeros_like(stf_ref)
    stb_ref[...] = jnp.zeros_like(stb_ref)
    nck = n // CHUNK
    ef, eb = _expand_matrix(False), _expand_matrix(True)
    for ci in range(nck):
        sl = slice(ci * CHUNK, (ci + 1) * CHUNK)
        _ssd_chunk(xbc[sl], dt_all[sl], a_all, ef, stf_ref.at[0], False, False)
    for ci in reversed(range(nck)):
        sl = slice(ci * CHUNK, (ci + 1) * CHUNK)
        _ssd_chunk(xbc[sl], dt_all[sl], a_all, eb, stb_ref.at[0], True, False)


def _ctx_call(ctx, shift, scale, norm_w, w_xd, conv_w, conv_b, dtb, alog):
    b, n, d = ctx.shape
    st_shape = jax.ShapeDtypeStruct((b, SSD_GROUPS, SSD_STATE, GROUP_W), F32)
    st_spec = pl.BlockSpec((1, SSD_GROUPS, SSD_STATE, GROUP_W), lambda i: (i, 0, 0, 0))
    full = lambda a: pl.BlockSpec(a.shape, lambda i: (0,) * a.ndim)
    return pl.pallas_call(
        _ctx_kernel,
        grid=(b,),
        in_specs=[pl.BlockSpec((1, n, d), lambda i: (i, 0, 0)),
                  full(shift), full(scale), full(norm_w), full(w_xd), full(conv_w), full(conv_b),
                  full(dtb), full(alog)],
        out_specs=(st_spec, st_spec),
        out_shape=(st_shape, st_shape),
        compiler_params=_cparams(1),
        name="ctx_state",
    )(ctx, shift, scale, norm_w, w_xd, conv_w, conv_b, dtb, alog)


def _inproj_kernel(x_ref, pos_ref, shift_ref, scale_ref, nw_ref, w_ref, gw_ref, gnw_ref, gb_ref,
                   xp_ref, z_ref, xbc_ref, dt_ref, yg_ref):
    xp = x_ref[0] + pos_ref[...]
    xp_ref[0] = xp
    h = _rms(xp, nw_ref[...]) * (1.0 + scale_ref[0]) + shift_ref[0]
    proj = _dot(h.astype(BF16), w_ref[...])
    z_ref[0] = proj[:, :SSD_WIDTH]
    xbc_ref[0] = proj[:, SSD_WIDTH:SSD_WIDTH + CONV_CH]
    o = SSD_WIDTH + CONV_CH
    u = jax.nn.gelu(proj[:, o:o + GMLP_WIDTH])
    v = jax.nn.gelu(proj[:, o + GMLP_WIDTH:o + 2 * GMLP_WIDTH])
    dt_ref[0] = proj[:, o + 2 * GMLP_WIDTH:]
    tm = u.shape[0]
    gnw = gnw_ref[...]
    gb = gb_ref[...]
    for g in range(GMLP_GROUPS):
        gs = slice(g * GMLP_GROUP_DIM, (g + 1) * GMLP_GROUP_DIM)
        vn = _rms(v[:, gs], gnw[:, gs]).astype(BF16)
        wg = gw_ref[g]
        for ci in range(tm // CHUNK):
            rs = slice(ci * CHUNK, (ci + 1) * CHUNK)
            mixed = _dot(wg, vn[rs]) + gb[:, gs]
            yg_ref[0, rs, gs] = (u[rs, gs] * mixed).astype(BF16)


def _inproj_call(x, pos, shift, scale, norm_w, w_main, gw, gnw, gb, tm):
    b, seq, d = x.shape
    nt = seq // tm
    full = lambda a: pl.BlockSpec(a.shape, lambda i, j: (0,) * a.ndim)
    tok = lambda w: pl.BlockSpec((1, tm, w), lambda i, j: (i, j, 0))
    per_b = pl.BlockSpec((1, 1, d), lambda i, j: (i, 0, 0))
    shp = lambda w, dt: jax.ShapeDtypeStruct((b, seq, w), dt)
    return pl.pallas_call(
        _inproj_kernel,
        grid=(b, nt),
        in_specs=[tok(d), pl.BlockSpec((tm, d), lambda i, j: (j, 0)), per_b, per_b, full(norm_w),
                  full(w_main), full(gw), full(gnw), full(gb)],
        out_specs=(tok(d), tok(SSD_WIDTH), tok(CONV_CH), tok(LANES), tok(GMLP_WIDTH)),
        out_shape=(shp(d, F32), shp(SSD_WIDTH, F32), shp(CONV_CH, F32), shp(LANES, F32),
                   shp(GMLP_WIDTH, BF16)),
        compiler_params=_cparams(2),
        name="inproj",
    )(x, pos, shift, scale, norm_w, w_main, gw, gnw, gb)


def _halo_rows(prev_ref, next_ref, c, nc):
    prev_row = jnp.where(c > 0, prev_ref[0, SUBLANES - 1:SUBLANES, :], 0.0)
    next_row = jnp.where(c < nc - 1, next_ref[0, 0:1, :], 0.0)
    return prev_row, next_row


def _ssd_fwd_kernel(xbc_ref, prev_ref, next_ref, dt_ref, cw_ref, cb_ref, dtb_ref, alog_ref,
                    dskip_ref, h0_ref, y_ref, h_ref):
    c = pl.program_id(1)
    nc = pl.num_programs(1)

    @pl.when(c == 0)
    def _():
        h_ref[...] = h0_ref[0]

    prev_row, next_row = _halo_rows(prev_ref, next_ref, c, nc)
    xbc = _conv_silu(xbc_ref[0], prev_row, next_row, cw_ref[...], cb_ref[...])
    dt_all = _softplus(dt_ref[0] + dtb_ref[...])
    a_all = -jnp.exp(alog_ref[...])
    expand = _expand_matrix(False)
    for ci in range(xbc.shape[0] // CHUNK):
        rs = slice(ci * CHUNK, (ci + 1) * CHUNK)
        y = _ssd_chunk(xbc[rs], dt_all[rs], a_all, expand, h_ref, False, True)
        y_ref[0, rs, :] = y + dskip_ref[...] * xbc[rs, :SSD_WIDTH]


def _ssd_bwd_kernel(xbc_ref, prev_ref, next_ref, dt_ref, cw_ref, cb_ref, dtb_ref, alog_ref,
                    h0_ref, yf_ref, z_ref, yg_ref, xp_ref, gate_ref, snw_ref, wo_ref,
                    o_ref, h_ref):
    cr = pl.program_id(1)
    nc = pl.num_programs(1)
    c = nc - 1 - cr

    @pl.when(cr == 0)
    def _():
        h_ref[...] = h0_ref[0]

    prev_row, next_row = _halo_rows(prev_ref, next_ref, c, nc)
    xbc = _conv_silu(xbc_ref[0], prev_row, next_row, cw_ref[...], cb_ref[...])
    dt_all = _softplus(dt_ref[0] + dtb_ref[...])
    a_all = -jnp.exp(alog_ref[...])
    expand = _expand_matrix(True)
    for ci in reversed(range(xbc.shape[0] // CHUNK)):
        rs = slice(ci * CHUNK, (ci + 1) * CHUNK)
        y = yf_ref[0, rs, :] + _ssd_chunk(xbc[rs], dt_all[rs], a_all, expand, h_ref, True, True)
        y_ssd = _rms(y * _silu(z_ref[0, rs, :]), snw_ref[...])
        out = (_dot(y_ssd.astype(BF16), wo_ref[:SSD_WIDTH, :])
               + _dot(yg_ref[0, rs, :], wo_ref[SSD_WIDTH:, :]))
        o_ref[0, rs, :] = xp_ref[0, rs, :] + gate_ref[0] * out


def _ssd_specs(seq, reverse):
    rows = min(SSD_STEP_ROWS, seq)
    nc = seq // rows
    hb = rows // SUBLANES
    last_hb = seq // SUBLANES - 1
    cidx = (lambda j: nc - 1 - j) if reverse else (lambda j: j)
    cur = lambda w: pl.BlockSpec((1, rows, w), lambda i, j: (i, cidx(j), 0))
    prev = pl.BlockSpec((1, SUBLANES, CONV_CH),
                        lambda i, j: (i, jnp.maximum(cidx(j) * hb - 1, 0), 0))
    nxt = pl.BlockSpec((1, SUBLANES, CONV_CH),
                       lambda i, j: (i, jnp.minimum((cidx(j) + 1) * hb, last_hb), 0))
    return nc, cur, prev, nxt


def _ssd_fwd_call(xbc_raw, dt_raw, conv_w, conv_b, dtb, alog, dskip, st_f):
    b, seq, _ = xbc_raw.shape
    nc, cur, prev, nxt = _ssd_specs(seq, False)
    full = lambda a: pl.BlockSpec(a.shape, lambda i, j: (0,) * a.ndim)
    st = pl.BlockSpec((1, SSD_GROUPS, SSD_STATE, GROUP_W), lambda i, j: (i, 0, 0, 0))
    return pl.pallas_call(
        _ssd_fwd_kernel,
        grid=(b, nc),
        in_specs=[cur(CONV_CH), prev, nxt, cur(LANES), full(conv_w), full(conv_b), full(dtb),
                  full(alog), full(dskip), st],
        out_specs=cur(SSD_WIDTH),
        out_shape=jax.ShapeDtypeStruct((b, seq, SSD_WIDTH), F32),
        scratch_shapes=[pltpu.VMEM((SSD_GROUPS, SSD_STATE, GROUP_W), F32)],
        compiler_params=_cparams(2),
        name="ssd_fwd",
    )(xbc_raw, xbc_raw, xbc_raw, dt_raw, conv_w, conv_b, dtb, alog, dskip, st_f)


def _ssd_bwd_call(xbc_raw, dt_raw, conv_w, conv_b, dtb, alog, st_b, y_f, z, y_g, xp, gate, snw, w_out):
    b, seq, d = xp.shape
    nc, cur, prev, nxt = _ssd_specs(seq, True)
    full = lambda a: pl.BlockSpec(a.shape, lambda i, j: (0,) * a.ndim)
    st = pl.BlockSpec((1, SSD_GROUPS, SSD_STATE, GROUP_W), lambda i, j: (i, 0, 0, 0))
    per_b = pl.BlockSpec((1, 1, d), lambda i, j: (i, 0, 0))
    return pl.pallas_call(
        _ssd_bwd_kernel,
        grid=(b, nc),
        in_specs=[cur(CONV_CH), prev, nxt, cur(LANES), full(conv_w), full(conv_b), full(dtb),
                  full(alog), st, cur(SSD_WIDTH), cur(SSD_WIDTH), cur(GMLP_WIDTH), cur(d), per_b,
                  full(snw), full(w_out)],
        out_specs=cur(d),
        out_shape=jax.ShapeDtypeStruct((b, seq, d), F32),
        scratch_shapes=[pltpu.VMEM((SSD_GROUPS, SSD_STATE, GROUP_W), F32)],
        compiler_params=_cparams(2),
        name="ssd_bwd",
    )(xbc_raw, xbc_raw, xbc_raw, dt_raw, conv_w, conv_b, dtb, alog, st_b, y_f, z, y_g, xp, gate,
      snw, w_out)


def _oem_sort_pairs(n):
    pairs = []
    p = 1
    while p < n:
        k = p
        while k >= 1:
            for j in range(k % p, n - k, 2 * k):
                for i in range(min(k, n - j - k)):
                    if (i + j) // (2 * p) == (i + j + k) // (2 * p):
                        pairs.append((i + j, i + j + k))
            k //= 2
        p *= 2
    return pairs


def _cmp_exchange(x, y):
    if x is None:
        return y, None
    if y is None:
        return x, None
    return jnp.maximum(x, y), jnp.minimum(x, y)


def _merge_top(xs, ys, n):
    xs = list(xs) + [None] * (n - len(xs))
    ys = list(ys) + [None] * (n - len(ys))
    out = [_cmp_exchange(xs[k], ys[n - 1 - k])[0] for k in range(n)]
    stride = n // 2
    while stride >= 1:
        for i in range(n):
            if i & stride == 0:
                out[i], out[i + stride] = _cmp_exchange(out[i], out[i + stride])
        stride //= 2
    return [v for v in out if v is not None]


def _top16_sorted(vals):
    groups = []
    for g in range(0, len(vals), PEER_TOPK):
        grp = list(vals[g:g + PEER_TOPK])
        for i, j in _oem_sort_pairs(PEER_TOPK):
            grp[i], grp[j] = _cmp_exchange(grp[i], grp[j])
        groups.append(grp)
    while len(groups) > 1:
        groups = [_merge_top(groups[i], groups[i + 1], PEER_TOPK) for i in range(0, len(groups), 2)]
    return groups[0]


def _peer_prep_kernel(x_ref, shift_ref, scale_ref, nw_ref, wqt_ref, keys_ref,
                      ht_ref, c1_ref, e1_ref, code2_ref, e2_ref, s_scr, o_scr):
    h = _rms(x_ref[...], nw_ref[...]) * (1.0 + scale_ref[0]) + shift_ref[0]
    ht = h.T.astype(BF16)
    ht_ref[...] = ht
    tp = ht.shape[1]
    ntg = tp // LANES
    assert ntg == SUBLANES
    qdim = 2 * PEER_HALF_DIM

    def head(hd, carry):
        qt = _dot(wqt_ref[pl.ds(pl.multiple_of(hd * qdim, qdim), qdim), :], ht)
        vals = []
        for k in range(2):
            st = _dot(keys_ref[k], qt[k * PEER_HALF_DIM:(k + 1) * PEER_HALF_DIM].astype(BF16))
            for tg in range(ntg):
                s_scr[k, tg * PEER_KEYS:(tg + 1) * PEER_KEYS, :] = st[:, tg * LANES:(tg + 1) * LANES]
            vals.append([s_scr[k, pl.ds(key, ntg, stride=PEER_KEYS), :] for key in range(PEER_KEYS)])
        a = _top16_sorted(vals[0])
        b = _top16_sorted(vals[1])
        rows = [[a[i] + b[j] for j in range(PEER_TOPK // (i + 1))] for i in range(PEER_TOPK // 2)]
        col0 = [a[i] + b[0] for i in range(PEER_TOPK // 2, PEER_TOPK)]
        top = _merge_top(rows[0], _merge_top(rows[1], col0, PEER_TOPK), PEER_TOPK)
        rest = _merge_top(_merge_top(rows[2], rows[3], PEER_TOPK),
                          _merge_top(_merge_top(rows[4], rows[5], PEER_TOPK),
                                     _merge_top(rows[6], rows[7], PEER_TOPK), PEER_TOPK), PEER_TOPK)
        top = _merge_top(top, rest, PEER_TOPK)
        thr = top[-1]
        zsum = sum(jnp.exp(t - top[0]) for t in top[1:]) + 1.0
        half_inv_z = 0.5 / zsum
        none = float(PEER_TOPK + 1)
        c_rank = []
        for i in range(PEER_TOPK):
            row = rows[i] if i < PEER_TOPK // 2 else [col0[i - PEER_TOPK // 2]]
            c_rank.append(none - sum(jnp.where(v >= thr, 1.0, 0.0) for v in row))
        for kb in range(0, PEER_KEYS, SUBLANES):
            keys = range(kb, kb + SUBLANES)
            c1 = {key: jnp.full_like(thr, none) for key in keys}
            code2 = {key: jnp.zeros_like(thr) for key in keys}
            for r in reversed(range(PEER_TOPK)):
                for key in keys:
                    c1[key] = jnp.where(vals[0][key] >= a[r], c_rank[r], c1[key])
                    code2[key] = jnp.where(vals[1][key] >= b[r], float(PEER_TOPK - r), code2[key])
            for key in keys:
                sl = slice(key * ntg, (key + 1) * ntg)
                o_scr[0, sl, :] = c1[key]
                o_scr[1, sl, :] = jnp.exp(vals[0][key] - a[0]) * half_inv_z
                o_scr[2, sl, :] = code2[key]
                o_scr[3, sl, :] = jnp.exp(vals[1][key] - b[0])
        for tg in range(ntg):
            rs = pl.ds(tg, PEER_KEYS, stride=ntg)
            ls = slice(tg * LANES, (tg + 1) * LANES)
            c1_ref[hd, tg] = o_scr[0, rs, :]
            e1_ref[hd, tg] = o_scr[1, rs, :]
            code2_ref[hd, :, ls] = o_scr[2, rs, :].astype(BF16)
            e2_ref[hd, :, ls] = o_scr[3, rs, :].astype(BF16)
        return carry

    lax.fori_loop(0, PEER_HEADS, head, 0)


def _peer_prep_call(x1, shift, scale, norm_w, w_qt, keys, tp, seq):
    t, d = x1.shape
    per_seq = seq // tp
    full = lambda a: pl.BlockSpec(a.shape, lambda i: (0,) * a.ndim)
    per_b = pl.BlockSpec((1, 1, d), lambda i: (i // per_seq, 0, 0))
    hk = pl.BlockSpec((PEER_HEADS, PEER_KEYS, tp), lambda i: (0, 0, i))
    hk_shape = lambda dt: jax.ShapeDtypeStruct((PEER_HEADS, PEER_KEYS, t), dt)
    hr = pl.BlockSpec((PEER_HEADS, tp // LANES, PEER_KEYS, LANES), lambda i: (0, i, 0, 0))
    hr_shape = jax.ShapeDtypeStruct((PEER_HEADS, t // LANES, PEER_KEYS, LANES), F32)
    rows = (tp // LANES) * PEER_KEYS
    return pl.pallas_call(
        _peer_prep_kernel,
        grid=(t // tp,),
        in_specs=[pl.BlockSpec((tp, d), lambda i: (i, 0)), per_b, per_b, full(norm_w), full(w_qt),
                  full(keys)],
        out_specs=(pl.BlockSpec((d, tp), lambda i: (0, i)), hr, hr, hk, hk),
        out_shape=(jax.ShapeDtypeStruct((d, t), BF16), hr_shape, hr_shape,
                   hk_shape(BF16), hk_shape(BF16)),
        scratch_shapes=[pltpu.VMEM((2, rows, LANES), F32), pltpu.VMEM((4, rows, LANES), F32)],
        compiler_params=_cparams(1),
        name="peer_prep",
    )(x1, shift, scale, norm_w, w_qt, keys)


def _gelu_x2(x):
    k = math.sqrt(2.0 / math.pi)
    inner = x * (x * x * (0.044715 * k) + k)
    return x * jnp.tanh(inner) + x


def _peer_dense_kernel(n_eb, ht_ref, u_ref, vt_ref, c1_ref, e1_ref, code2_ref, e2_ref, x_ref,
                       gate_ref, fnw_ref, o_ref, acc_ref, act_a_ref, act_b_ref, wg_ref):
    s = pl.program_id(0)
    jv = lax.rem(jnp.maximum(s - 1, 0), n_eb)
    slot = lax.rem(s, 2)

    @pl.when(s == 0)
    def _():
        act_b_ref[...] = jnp.zeros_like(act_b_ref)

    @pl.when(jv == 0)
    def _():
        acc_ref[...] = jnp.zeros_like(acc_ref)

    tb = act_a_ref.shape[1]
    n_tiles = PEER_KEYS // BF16_ROWS

    def step(act_new, act_prev):
        for i1 in range(c1_ref.shape[2]):
            rows = slice(i1 * PEER_KEYS, (i1 + 1) * PEER_KEYS)
            act_rows = _gelu_x2(_dot(u_ref[rows, :], ht_ref[...])).astype(BF16)
            bcast = pl.ds(i1, BF16_ROWS, stride=0)
            for g0 in range(0, tb // LANES, GATE_LANE_GROUPS):
                lts = range(g0, g0 + GATE_LANE_GROUPS)
                ls = slice(g0 * LANES, (g0 + GATE_LANE_GROUPS) * LANES)
                w = [None] * n_tiles
                for hd in range(PEER_HEADS):
                    c_row = jnp.concatenate([c1_ref[hd, lt, bcast, :] for lt in lts],
                                            axis=1).astype(BF16)
                    e_row = jnp.concatenate([e1_ref[hd, lt, bcast, :] for lt in lts],
                                            axis=1).astype(BF16)
                    for it in range(n_tiles):
                        rs = slice(it * BF16_ROWS, (it + 1) * BF16_ROWS)
                        sel = code2_ref[hd, rs, ls] >= c_row
                        term = jnp.where(sel, e2_ref[hd, rs, ls] * e_row, jnp.zeros_like(e_row))
                        w[it] = term if w[it] is None else w[it] + term
                wg_ref[rows, ls] = jnp.concatenate(w, axis=0) * act_prev[rows, ls]
            act_new[rows, :] = act_rows
        acc_ref[...] += _dot(vt_ref[...], wg_ref[...])

    pl.when(slot == 0)(lambda: step(act_a_ref, act_b_ref))
    pl.when(slot == 1)(lambda: step(act_b_ref, act_a_ref))

    @pl.when(jnp.logical_and(s > 0, jv == n_eb - 1))
    def _():
        x2 = x_ref[...] + gate_ref[0] * acc_ref[...].T
        o_ref[...] = _rms(x2, fnw_ref[...])


def _peer_dense_call(ht, u_b, vt_b, c1, e1, code2, e2, x1, gate, fnw, tb, eb, seq):
    d, t = ht.shape
    e = u_b.shape[0]
    per_seq = seq // tb
    n_i1 = eb // PEER_KEYS
    n_eb = e // eb
    total = (t // tb) * n_eb
    ti1 = lambda s: jnp.minimum(s, total - 1) // n_eb
    ej1 = lambda s: jnp.minimum(s, total - 1) % n_eb
    ti2 = lambda s: jnp.maximum(s - 1, 0) // n_eb
    ej2 = lambda s: jnp.maximum(s - 1, 0) % n_eb
    rows1 = pl.BlockSpec((PEER_HEADS, tb // LANES, n_i1, LANES), lambda s: (0, ti2(s), ej2(s), 0))
    rows2 = pl.BlockSpec((PEER_HEADS, PEER_KEYS, tb), lambda s: (0, 0, ti2(s)))
    return pl.pallas_call(
        functools.partial(_peer_dense_kernel, n_eb),
        grid=(total + 1,),
        in_specs=[pl.BlockSpec((d, tb), lambda s: (0, ti1(s))),
                  pl.BlockSpec((eb, d), lambda s: (ej1(s), 0)),
                  pl.BlockSpec((d, eb), lambda s: (0, ej2(s))),
                  rows1, rows1, rows2, rows2,
                  pl.BlockSpec((tb, d), lambda s: (ti2(s), 0), pipeline_mode=pl.Buffered(1)),
                  pl.BlockSpec((1, 1, d), lambda s: (ti2(s) // per_seq, 0, 0)),
                  pl.BlockSpec((1, d), lambda s: (0, 0))],
        out_specs=pl.BlockSpec((tb, d), lambda s: (ti2(s), 0)),
        out_shape=jax.ShapeDtypeStruct((t, d), F32),
        scratch_shapes=[pltpu.VMEM((d, tb), F32), pltpu.VMEM((eb, tb), BF16),
                        pltpu.VMEM((eb, tb), BF16), pltpu.VMEM((eb, tb), BF16)],
        compiler_params=_cparams(1),
        name="peer_dense",
    )(ht, u_b, vt_b, c1, e1, code2, e2, x1, gate, fnw)


def _tile_sizes(seq):
    tm = min(512, seq)
    return tm, SUBLANES * LANES, min(512, seq), 2048


def kernel(x, c, ctx, c_ctx, w_mod, b_mod, norm1_w, w_in, conv_w, conv_b, dt_bias, a_log, d_skip,
           ssd_norm_w, gmlp_norm_w, gmlp_ws, gmlp_bs, w_out, norm2_w, peer_wq, peer_keys, peer_u,
           peer_v, final_norm_w):
    b, seq, d = x.shape
    assert w_mod.shape[0] == 1, "single-layer block"
    assert seq % CHUNK == 0 and ctx.shape[1] % CHUNK == 0 and b + 1 <= SUBLANES
    tm, tp, tb, eb = _tile_sizes(seq)
    assert seq % tm == 0 and seq % tp == 0 and seq % tb == 0 and seq % SSD_STEP_ROWS == 0

    cc = jnp.zeros((SUBLANES, d), F32).at[:b].set(c).at[b].set(c_ctx)
    mod = _mod_call(cc, w_mod[0], b_mod[0][None, :]).reshape(SUBLANES, 6, d)
    mod_x = [mod[:b, k][:, None, :] for k in range(6)]
    shift_s, scale_s = mod[b, 0][None, :], mod[b, 1][None, :]

    nf = d // 4
    omega = (1.0 / (10000.0 ** (jnp.arange(nf, dtype=F32) / nf)))[None, :]
    pos = _pos_call(seq, d, omega)

    w = w_in[0]
    o1, o2 = SSD_WIDTH, SSD_WIDTH + CONV_CH
    o3 = o2 + 2 * SSD_HEADS
    dt_pad = jnp.zeros((d, LANES - 2 * SSD_HEADS), F32)
    w_dt = jnp.concatenate([w[:, o2:o3], dt_pad], axis=1)
    w_main = jnp.concatenate([w[:, :o2], w[:, o3:], w_dt], axis=1).astype(BF16)
    w_xd = jnp.concatenate([w[:, o1:o2], w_dt], axis=1).astype(BF16)
    pad1 = jnp.zeros((1, LANES - 2 * SSD_HEADS), F32)
    dtb = jnp.concatenate([dt_bias[0].reshape(1, -1), pad1], axis=1)
    alog = jnp.concatenate([a_log[0].reshape(1, -1), pad1], axis=1)
    dskip = jnp.repeat(d_skip[0], SSD_HEAD_DIM)[None, :]
    cw, cb = conv_w[0], conv_b[0][None, :]
    n1 = norm1_w[0][None, :]

    st_f, st_b = _ctx_call(ctx, shift_s, scale_s, n1, w_xd, cw, cb, dtb, alog)

    gw = gmlp_ws[0].astype(BF16)
    gnw = gmlp_norm_w[0].reshape(1, GMLP_WIDTH)
    gb = jnp.repeat(gmlp_bs[0].T, GMLP_GROUP_DIM, axis=1)
    xp, z, xbc_raw, dt_raw, y_g = _inproj_call(x, pos, mod_x[0], mod_x[1], n1, w_main, gw, gnw, gb, tm)

    y_f = _ssd_fwd_call(xbc_raw, dt_raw, cw, cb, dtb, alog, dskip, st_f)
    x1 = _ssd_bwd_call(xbc_raw, dt_raw, cw, cb, dtb, alog, st_b, y_f, z, y_g, xp, mod_x[2],
                       ssd_norm_w[0][None, :], w_out[0].astype(BF16))

    x1f = x1.reshape(b * seq, d)
    ht, c1, e1, code2, e2 = _peer_prep_call(
        x1f, mod_x[3], mod_x[4], norm2_w[0][None, :], peer_wq[0].T.astype(BF16),
        peer_keys[0].astype(BF16), tp, seq)
    out = _peer_dense_call(ht, peer_u[0].astype(BF16), peer_v[0].T.astype(BF16), c1, e1, code2, e2,
                           x1f, mod_x[5], final_norm_w[None, :], tb, eb, seq)
    return out.reshape(b, seq, d)
```

```python
import functools
import math

import jax
import jax.numpy as jnp
from jax import lax
from jax.experimental import pallas as pl
from jax.experimental.pallas import tpu as pltpu

F32 = jnp.float32
BF16 = jnp.bfloat16

GRID_W = 64
SSD_HEAD_DIM = 64
SSD_HEADS = 16
SSD_GROUPS = 2
HEADS_PER_GROUP = SSD_HEADS // SSD_GROUPS
SSD_STATE = 128
CHUNK = 128
SSD_WIDTH = SSD_HEADS * SSD_HEAD_DIM
GN = SSD_GROUPS * SSD_STATE
CONV_CH = SSD_WIDTH + 2 * GN
GMLP_GROUPS = 8
GMLP_GROUP_DIM = 128
GMLP_WIDTH = GMLP_GROUPS * GMLP_GROUP_DIM
PEER_KEYS = 128
PEER_HEADS = 8
PEER_HALF_DIM = 128
PEER_TOPK = 16
EPS = 1e-6

LANES = 128
SUBLANES = 8
BF16_ROWS = 16
VMEM_LIMIT = 56 * 1024 * 1024

GROUP_W = HEADS_PER_GROUP * SSD_HEAD_DIM
SSD_STEP_ROWS = 4 * CHUNK
GATE_LANE_GROUPS = 4
ACT_DOT_ROWS = 128


def _cparams(n_axes, flags=None):
    return pltpu.CompilerParams(
        dimension_semantics=("arbitrary",) * n_axes,
        vmem_limit_bytes=VMEM_LIMIT, flags=flags)


def _rms(x, w):
    return x * lax.rsqrt(jnp.mean(x * x, axis=-1, keepdims=True) + EPS) * w


def _silu(x):
    return x * jax.nn.sigmoid(x)


def _dot(a, b):
    return jnp.dot(a, b, preferred_element_type=F32)


def _dot_nt(a, b):
    return lax.dot_general(a, b, (((1,), (1,)), ((), ())), preferred_element_type=F32)


def _mod_kernel(c_ref, w_ref, b_ref, o_ref):
    a = _silu(c_ref[...]).astype(BF16)
    o_ref[...] = _dot(a, w_ref[...].astype(BF16)) + b_ref[...]


def _mod_call(cc, w_mod, b_mod):
    d, n = w_mod.shape
    bn = 1536
    return pl.pallas_call(
        _mod_kernel,
        grid=(n // bn,),
        in_specs=[pl.BlockSpec((SUBLANES, d), lambda j: (0, 0)),
                  pl.BlockSpec((d, bn), lambda j: (0, j)),
                  pl.BlockSpec((1, bn), lambda j: (0, j))],
        out_specs=pl.BlockSpec((SUBLANES, bn), lambda j: (0, j)),
        out_shape=jax.ShapeDtypeStruct((SUBLANES, n), F32),
        compiler_params=_cparams(1),
        name="mod",
    )(cc, w_mod, b_mod)


def _pos_kernel(omega_ref, row_ref, col_ref):
    nf = omega_ref.shape[1]
    om = omega_ref[...]
    for ref in (row_ref, col_ref):
        ang = lax.broadcasted_iota(jnp.int32, (ref.shape[0], nf), 0).astype(F32) * om
        ref[:, :nf] = jnp.sin(ang)
        ref[:, nf:] = jnp.cos(ang)


def _pos_call(seq, dim, omega):
    half = dim // 2
    return pl.pallas_call(
        _pos_kernel,
        out_shape=(jax.ShapeDtypeStruct((seq // GRID_W, half), F32),
                   jax.ShapeDtypeStruct((GRID_W, half), F32)),
        name="pos",
    )(omega)


def _conv_silu(x, prev_row, next_row, w, b):
    n = x.shape[0]
    rows = lax.broadcasted_iota(jnp.int32, x.shape, 0)
    x_prev = jnp.where(rows == 0, prev_row, pltpu.roll(x, 1, axis=0))
    x_next = jnp.where(rows == n - 1, next_row, pltpu.roll(x, n - 1, axis=0))
    y = b + x_prev * w[0:1, :] + x * w[1:2, :] + x_next * w[2:3, :]
    return _silu(y)


def _softplus(x):
    return jnp.maximum(x, 0.0) + jnp.log1p(jnp.exp(-jnp.abs(x)))


def _split3(x):
    hi = x.astype(BF16)
    r = x - hi.astype(F32)
    mid = r.astype(BF16)
    return hi, mid, (r - mid.astype(F32)).astype(BF16)


def _dot_exact_rhs(x, m):
    return sum(_dot(p, m) for p in _split3(x))


def _dot_exact_lhs(m, x):
    return sum(_dot(m, p) for p in _split3(x))


def _ssd_chunk(xbc, dt_all, a_all, expand, h_ref, reverse, want_y):
    q = xbc.shape[0]
    col0 = SSD_HEADS if reverse else 0
    xs = xbc[:, :SSD_WIDTH]
    d_a = dt_all * a_all
    ti = lax.broadcasted_iota(jnp.int32, (q, q), 0)
    si = lax.broadcasted_iota(jnp.int32, (q, q), 1)
    keep = (si >= ti) if reverse else (si <= ti)
    tri = jnp.where(keep, 1.0, 0.0).astype(BF16)
    acs = _dot_exact_lhs(tri, d_a)
    last = 0 if reverse else q - 1
    total = acs[last:last + 1, :]
    stacked = jnp.concatenate([jnp.exp(acs), dt_all * jnp.exp(total - acs)], axis=0)
    ex = _dot_exact_rhs(stacked, expand)
    ea_x = ex[:q]
    xd = (xs * ex[q:]).astype(BF16)
    xs_b = xs.astype(BF16)
    chunk_decay = ea_x[last:last + 1, :]
    acs_t = acs.T if want_y else None
    dt_t = dt_all.T if want_y else None
    lane = lax.broadcasted_iota(jnp.int32, (q, LANES), 1)
    ys = []
    for g in range(SSD_GROUPS):
        bm = xbc[:, SSD_WIDTH + g * SSD_STATE:SSD_WIDTH + (g + 1) * SSD_STATE]
        cm = xbc[:, SSD_WIDTH + GN + g * SSD_STATE:SSD_WIDTH + GN + (g + 1) * SSD_STATE]
        gsl = slice(g * GROUP_W, (g + 1) * GROUP_W)
        h_old = h_ref[g]
        if want_y:
            cm_b = cm.astype(BF16)
            cb = _dot_nt(cm_b, bm.astype(BF16))
            y_off = _dot(cm_b, h_old.astype(BF16)) * ea_x[:, gsl]
            pieces = []
            for pr in range(HEADS_PER_GROUP // 2):
                ms = []
                for r in (2 * pr, 2 * pr + 1):
                    j = col0 + g * HEADS_PER_GROUP + r
                    seg = acs[:, j:j + 1] - acs_t[j:j + 1, :]
                    lm = jnp.exp(jnp.where(keep, seg, -jnp.inf))
                    ms.append((cb * lm * dt_t[j:j + 1, :]).astype(BF16))
                c0 = g * GROUP_W + pr * LANES
                xp = xs_b[:, c0:c0 + LANES]
                zero = jnp.zeros_like(xp)
                rhs = jnp.concatenate([jnp.where(lane < SSD_HEAD_DIM, xp, zero),
                                       jnp.where(lane >= SSD_HEAD_DIM, xp, zero)], axis=0)
                pieces.append(_dot(jnp.concatenate(ms, axis=1), rhs))
            ys.append(jnp.concatenate(pieces, axis=1) + y_off)
        h_ref[g] = h_old * chunk_decay[:, gsl] + _dot(bm.T.astype(BF16), xd[:, gsl])
    return jnp.concatenate(ys, axis=1) if want_y else None


def _expand_matrix(reverse):
    rows = lax.broadcasted_iota(jnp.int32, (LANES, SSD_WIDTH), 0)
    cols = lax.broadcasted_iota(jnp.int32, (LANES, SSD_WIDTH), 1)
    head = cols // SSD_HEAD_DIM + (SSD_HEADS if reverse else 0)
    return jnp.where(rows == head, 1.0, 0.0).astype(BF16)


def _ctx_kernel(ctx_ref, shift_ref, scale_ref, nw_ref, w_ref, cw_ref, cb_ref, dtb_ref, alog_ref,
                stf_ref, stb_ref):
    h = _rms(ctx_ref[0], nw_ref[...]) * (1.0 + scale_ref[...]) + shift_ref[...]
    proj = _dot(h.astype(BF16), w_ref[...])
    n = proj.shape[0]
    zero_row = jnp.zeros((1, CONV_CH), F32)
    xbc = _conv_silu(proj[:, :CONV_CH], zero_row, zero_row, cw_ref[...], cb_ref[...])
    dt_all = _softplus(proj[:, CONV_CH:] + dtb_ref[...])
    a_all = -jnp.exp(alog_ref[...])
    stf_ref[...] = jnp.zeros_like(stf_ref)
    stb_ref[...] = jnp.zeros_like(stb_ref)
    nck = n // CHUNK
    ef, eb = _expand_matrix(False), _expand_matrix(True)
    for ci in range(nck):
        sl = slice(ci * CHUNK, (ci + 1) * CHUNK)
        _ssd_chunk(xbc[sl], dt_all[sl], a_all, ef, stf_ref.at[0], False, False)
    for ci in reversed(range(nck)):
        sl = slice(ci * CHUNK, (ci + 1) * CHUNK)
        _ssd_chunk(xbc[sl], dt_all[sl], a_all, eb, stb_ref.at[0], True, False)


def _ctx_call(ctx, shift, scale, norm_w, w_xd, conv_w, conv_b, dtb, alog):
    b, n, d = ctx.shape
    st_shape = jax.ShapeDtypeStruct((b, SSD_GROUPS, SSD_STATE, GROUP_W), F32)
    st_spec = pl.BlockSpec((1, SSD_GROUPS, SSD_STATE, GROUP_W), lambda i: (i, 0, 0, 0))
    full = lambda a: pl.BlockSpec(a.shape, lambda i: (0,) * a.ndim)
    return pl.pallas_call(
        _ctx_kernel,
        grid=(b,),
        in_specs=[pl.BlockSpec((1, n, d), lambda i: (i, 0, 0)),
                  full(shift), full(scale), full(norm_w), full(w_xd), full(conv_w), full(conv_b),
                  full(dtb), full(alog)],
        out_specs=(st_spec, st_spec),
        out_shape=(st_shape, st_shape),
        compiler_params=_cparams(1),
        name="ctx_state",
    )(ctx, shift, scale, norm_w, w_xd, conv_w, conv_b, dtb, alog)


def _inproj_kernel(x_ref, prow_ref, pcol_ref, shift_ref, scale_ref, nw_ref, w_ref, gw_ref, gnw_ref,
                   gb_ref, xp_ref, z_ref, xbc_ref, dt_ref, yg_ref):
    pcol = pcol_ref[...]
    pos = jnp.concatenate(
        [jnp.concatenate([jnp.broadcast_to(prow_ref[k:k + 1, :], pcol.shape), pcol], axis=1)
         for k in range(prow_ref.shape[0])], axis=0)
    xp = x_ref[0] + pos
    xp_ref[0] = xp
    h = _rms(xp, nw_ref[...]) * (1.0 + scale_ref[0]) + shift_ref[0]
    proj = _dot(h.astype(BF16), w_ref[...])
    z_ref[0] = proj[:, :SSD_WIDTH]
    xbc_ref[0] = proj[:, SSD_WIDTH:SSD_WIDTH + CONV_CH]
    o = SSD_WIDTH + CONV_CH
    u = jax.nn.gelu(proj[:, o:o + GMLP_WIDTH])
    v = jax.nn.gelu(proj[:, o + GMLP_WIDTH:o + 2 * GMLP_WIDTH])
    dt_ref[0] = proj[:, o + 2 * GMLP_WIDTH:]
    tm = u.shape[0]
    gnw = gnw_ref[...]
    gb = gb_ref[...]
    for g in range(GMLP_GROUPS):
        gs = slice(g * GMLP_GROUP_DIM, (g + 1) * GMLP_GROUP_DIM)
        vn = _rms(v[:, gs], gnw[:, gs]).astype(BF16)
        wg = gw_ref[g]
        for ci in range(tm // CHUNK):
            rs = slice(ci * CHUNK, (ci + 1) * CHUNK)
            mixed = _dot(wg, vn[rs]) + gb[:, gs]
            yg_ref[0, rs, gs] = (u[rs, gs] * mixed).astype(BF16)


def _inproj_call(x, pos_row, pos_col, shift, scale, norm_w, w_main, gw, gnw, gb, tm):
    b, seq, d = x.shape
    nt = seq // tm
    assert tm % (SUBLANES * GRID_W) == 0
    full = lambda a: pl.BlockSpec(a.shape, lambda i, j: (0,) * a.ndim)
    tok = lambda w: pl.BlockSpec((1, tm, w), lambda i, j: (i, j, 0))
    per_b = pl.BlockSpec((1, 1, d), lambda i, j: (i, 0, 0))
    shp = lambda w, dt: jax.ShapeDtypeStruct((b, seq, w), dt)
    return pl.pallas_call(
        _inproj_kernel,
        grid=(b, nt),
        in_specs=[tok(d), pl.BlockSpec((tm // GRID_W, d // 2), lambda i, j: (j, 0)), full(pos_col),
                  per_b, per_b, full(norm_w), full(w_main), full(gw), full(gnw), full(gb)],
        out_specs=(tok(d), tok(SSD_WIDTH), tok(CONV_CH), tok(LANES), tok(GMLP_WIDTH)),
        out_shape=(shp(d, F32), shp(SSD_WIDTH, F32), shp(CONV_CH, F32), shp(LANES, F32),
                   shp(GMLP_WIDTH, BF16)),
        compiler_params=_cparams(2),
        name="inproj",
    )(x, pos_row, pos_col, shift, scale, norm_w, w_main, gw, gnw, gb)


def _halo_rows(prev_ref, next_ref, c, nc):
    prev_row = jnp.where(c > 0, prev_ref[0, SUBLANES - 1:SUBLANES, :], 0.0)
    next_row = jnp.where(c < nc - 1, next_ref[0, 0:1, :], 0.0)
    return prev_row, next_row


def _ssd_fwd_kernel(xbc_ref, prev_ref, next_ref, dt_ref, cw_ref, cb_ref, dtb_ref, alog_ref,
                    dskip_ref, h0_ref, y_ref, h_ref):
    c = pl.program_id(1)
    nc = pl.num_programs(1)

    @pl.when(c == 0)
    def _():
        h_ref[...] = h0_ref[0]

    prev_row, next_row = _halo_rows(prev_ref, next_ref, c, nc)
    xbc = _conv_silu(xbc_ref[0], prev_row, next_row, cw_ref[...], cb_ref[...])
    dt_all = _softplus(dt_ref[0] + dtb_ref[...])
    a_all = -jnp.exp(alog_ref[...])
    expand = _expand_matrix(False)
    for ci in range(xbc.shape[0] // CHUNK):
        rs = slice(ci * CHUNK, (ci + 1) * CHUNK)
        y = _ssd_chunk(xbc[rs], dt_all[rs], a_all, expand, h_ref, False, True)
        y_ref[0, rs, :] = y + dskip_ref[...] * xbc[rs, :SSD_WIDTH]


def _ssd_bwd_kernel(xbc_ref, prev_ref, next_ref, dt_ref, cw_ref, cb_ref, dtb_ref, alog_ref,
                    h0_ref, yf_ref, z_ref, yg_ref, xp_ref, gate_ref, snw_ref, wo_ref,
                    o_ref, h_ref):
    cr = pl.program_id(1)
    nc = pl.num_programs(1)
    c = nc - 1 - cr

    @pl.when(cr == 0)
    def _():
        h_ref[...] = h0_ref[0]

    prev_row, next_row = _halo_rows(prev_ref, next_ref, c, nc)
    xbc = _conv_silu(xbc_ref[0], prev_row, next_row, cw_ref[...], cb_ref[...])
    dt_all = _softplus(dt_ref[0] + dtb_ref[...])
    a_all = -jnp.exp(alog_ref[...])
    expand = _expand_matrix(True)
    for ci in reversed(range(xbc.shape[0] // CHUNK)):
        rs = slice(ci * CHUNK, (ci + 1) * CHUNK)
        y = yf_ref[0, rs, :] + _ssd_chunk(xbc[rs], dt_all[rs], a_all, expand, h_ref, True, True)
        y_ssd = _rms(y * _silu(z_ref[0, rs, :]), snw_ref[...])
        out = (_dot(y_ssd.astype(BF16), wo_ref[:SSD_WIDTH, :])
               + _dot(yg_ref[0, rs, :], wo_ref[SSD_WIDTH:, :]))
        o_ref[0, rs, :] = xp_ref[0, rs, :] + gate_ref[0] * out


def _ssd_specs(seq, reverse):
    rows = min(SSD_STEP_ROWS, seq)
    nc = seq // rows
    hb = rows // SUBLANES
    last_hb = seq // SUBLANES - 1
    cidx = (lambda j: nc - 1 - j) if reverse else (lambda j: j)
    cur = lambda w: pl.BlockSpec((1, rows, w), lambda i, j: (i, cidx(j), 0))
    prev = pl.BlockSpec((1, SUBLANES, CONV_CH),
                        lambda i, j: (i, jnp.maximum(cidx(j) * hb - 1, 0), 0))
    nxt = pl.BlockSpec((1, SUBLANES, CONV_CH),
                       lambda i, j: (i, jnp.minimum((cidx(j) + 1) * hb, last_hb), 0))
    return nc, cur, prev, nxt


def _ssd_fwd_call(xbc_raw, dt_raw, conv_w, conv_b, dtb, alog, dskip, st_f):
    b, seq, _ = xbc_raw.shape
    nc, cur, prev, nxt = _ssd_specs(seq, False)
    full = lambda a: pl.BlockSpec(a.shape, lambda i, j: (0,) * a.ndim)
    st = pl.BlockSpec((1, SSD_GROUPS, SSD_STATE, GROUP_W), lambda i, j: (i, 0, 0, 0))
    return pl.pallas_call(
        _ssd_fwd_kernel,
        grid=(b, nc),
        in_specs=[cur(CONV_CH), prev, nxt, cur(LANES), full(conv_w), full(conv_b), full(dtb),
                  full(alog), full(dskip), st],
        out_specs=cur(SSD_WIDTH),
        out_shape=jax.ShapeDtypeStruct((b, seq, SSD_WIDTH), F32),
        scratch_shapes=[pltpu.VMEM((SSD_GROUPS, SSD_STATE, GROUP_W), F32)],
        compiler_params=_cparams(2),
        name="ssd_fwd",
    )(xbc_raw, xbc_raw, xbc_raw, dt_raw, conv_w, conv_b, dtb, alog, dskip, st_f)


def _ssd_bwd_call(xbc_raw, dt_raw, conv_w, conv_b, dtb, alog, st_b, y_f, z, y_g, xp, gate, snw, w_out):
    b, seq, d = xp.shape
    nc, cur, prev, nxt = _ssd_specs(seq, True)
    full = lambda a: pl.BlockSpec(a.shape, lambda i, j: (0,) * a.ndim)
    st = pl.BlockSpec((1, SSD_GROUPS, SSD_STATE, GROUP_W), lambda i, j: (i, 0, 0, 0))
    per_b = pl.BlockSpec((1, 1, d), lambda i, j: (i, 0, 0))
    return pl.pallas_call(
        _ssd_bwd_kernel,
        grid=(b, nc),
        in_specs=[cur(CONV_CH), prev, nxt, cur(LANES), full(conv_w), full(conv_b), full(dtb),
                  full(alog), st, cur(SSD_WIDTH), cur(SSD_WIDTH), cur(GMLP_WIDTH), cur(d), per_b,
                  full(snw), full(w_out)],
        out_specs=cur(d),
        out_shape=jax.ShapeDtypeStruct((b, seq, d), F32),
        scratch_shapes=[pltpu.VMEM((SSD_GROUPS, SSD_STATE, GROUP_W), F32)],
        compiler_params=_cparams(2),
        name="ssd_bwd",
    )(xbc_raw, xbc_raw, xbc_raw, dt_raw, conv_w, conv_b, dtb, alog, st_b, y_f, z, y_g, xp, gate,
      snw, w_out)


def _oem_sort_pairs(n):
    pairs = []
    p = 1
    while p < n:
        k = p
        while k >= 1:
            for j in range(k % p, n - k, 2 * k):
                for i in range(min(k, n - j - k)):
                    if (i + j) // (2 * p) == (i + j + k) // (2 * p):
                        pairs.append((i + j, i + j + k))
            k //= 2
        p *= 2
    return pairs


def _cmp_exchange(x, y):
    if x is None:
        return y, None
    if y is None:
        return x, None
    return jnp.maximum(x, y), jnp.minimum(x, y)


def _merge_top(xs, ys, n):
    xs = list(xs) + [None] * (n - len(xs))
    ys = list(ys) + [None] * (n - len(ys))
    out = [_cmp_exchange(xs[k], ys[n - 1 - k])[0] for k in range(n)]
    stride = n // 2
    while stride >= 1:
        for i in range(n):
            if i & stride == 0:
                out[i], out[i + stride] = _cmp_exchange(out[i], out[i + stride])
        stride //= 2
    return [v for v in out if v is not None]


def _top16_sorted(vals):
    groups = []
    for g in range(0, len(vals), PEER_TOPK):
        grp = list(vals[g:g + PEER_TOPK])
        for i, j in _oem_sort_pairs(PEER_TOPK):
            grp[i], grp[j] = _cmp_exchange(grp[i], grp[j])
        groups.append(grp)
    while len(groups) > 1:
        groups = [_merge_top(groups[i], groups[i + 1], PEER_TOPK) for i in range(0, len(groups), 2)]
    return groups[0]


def _peer_prep_kernel(x_ref, shift_ref, scale_ref, nw_ref, wqt_ref, keys_ref,
                      ht_ref, c1_ref, e1_ref, code2_ref, e2_ref, q_scr, s_scr, o_scr):
    h = _rms(x_ref[...], nw_ref[...]) * (1.0 + scale_ref[0]) + shift_ref[0]
    ht = h.T.astype(BF16)
    ht_ref[...] = ht
    tp = ht.shape[1]
    ntg = tp // LANES
    assert ntg == SUBLANES
    q_rows_per_dot = 4 * PEER_HALF_DIM
    for r0 in range(0, q_scr.shape[0], q_rows_per_dot):
        q_scr[r0:r0 + q_rows_per_dot, :] = _dot(wqt_ref[r0:r0 + q_rows_per_dot, :], ht).astype(BF16)

    def head(hd, carry):
        vals = []
        for k in range(2):
            q_rows = pl.ds(pl.multiple_of((hd * 2 + k) * PEER_HALF_DIM, PEER_HALF_DIM), PEER_HALF_DIM)
            st = _dot(keys_ref[k], q_scr[q_rows, :])
            for tg in range(ntg):
                s_scr[k, tg * PEER_KEYS:(tg + 1) * PEER_KEYS, :] = st[:, tg * LANES:(tg + 1) * LANES]
            vals.append([s_scr[k, pl.ds(key, ntg, stride=PEER_KEYS), :] for key in range(PEER_KEYS)])
        a = _top16_sorted(vals[0])
        b = _top16_sorted(vals[1])
        rows = [[a[i] + b[j] for j in range(PEER_TOPK // (i + 1))] for i in range(PEER_TOPK // 2)]
        col0 = [a[i] + b[0] for i in range(PEER_TOPK // 2, PEER_TOPK)]
        top = _merge_top(rows[0], _merge_top(rows[1], col0, PEER_TOPK), PEER_TOPK)
        rest = _merge_top(_merge_top(rows[2], rows[3], PEER_TOPK),
                          _merge_top(_merge_top(rows[4], rows[5], PEER_TOPK),
                                     _merge_top(rows[6], rows[7], PEER_TOPK), PEER_TOPK), PEER_TOPK)
        top = _merge_top(top, rest, PEER_TOPK)
        thr = top[-1]
        zsum = sum(jnp.exp(t - top[0]) for t in top[1:]) + 1.0
        half_inv_z = 0.5 / zsum
        none = float(PEER_TOPK + 1)
        c_rank = []
        for i in range(PEER_TOPK):
            row = rows[i] if i < PEER_TOPK // 2 else [col0[i - PEER_TOPK // 2]]
            c_rank.append(none - sum(jnp.where(v >= thr, 1.0, 0.0) for v in row))
        for kb in range(0, PEER_KEYS, SUBLANES):
            keys = range(kb, kb + SUBLANES)
            c1 = {key: jnp.full_like(thr, none) for key in keys}
            code2 = {key: jnp.zeros_like(thr) for key in keys}
            for r in reversed(range(PEER_TOPK)):
                for key in keys:
                    c1[key] = jnp.where(vals[0][key] >= a[r], c_rank[r], c1[key])
                    code2[key] = jnp.where(vals[1][key] >= b[r], float(PEER_TOPK - r), code2[key])
            for key in keys:
                sl = slice(key * ntg, (key + 1) * ntg)
                o_scr[0, sl, :] = c1[key]
                o_scr[1, sl, :] = jnp.exp(vals[0][key] - a[0]) * half_inv_z
                o_scr[2, sl, :] = code2[key]
                o_scr[3, sl, :] = jnp.exp(vals[1][key] - b[0])
        for tg in range(ntg):
            rs = pl.ds(tg, PEER_KEYS, stride=ntg)
            ls = slice(tg * LANES, (tg + 1) * LANES)
            c1_ref[hd, tg] = o_scr[0, rs, :]
            e1_ref[hd, tg] = o_scr[1, rs, :]
            code2_ref[hd, :, ls] = o_scr[2, rs, :].astype(BF16)
            e2_ref[hd, :, ls] = o_scr[3, rs, :].astype(BF16)
        return carry

    lax.fori_loop(0, PEER_HEADS, head, 0)


def _peer_prep_call(x1, shift, scale, norm_w, w_qt, keys, tp, seq):
    t, d = x1.shape
    per_seq = seq // tp
    full = lambda a: pl.BlockSpec(a.shape, lambda i: (0,) * a.ndim)
    per_b = pl.BlockSpec((1, 1, d), lambda i: (i // per_seq, 0, 0))
    hk = pl.BlockSpec((PEER_HEADS, PEER_KEYS, tp), lambda i: (0, 0, i))
    hk_shape = lambda dt: jax.ShapeDtypeStruct((PEER_HEADS, PEER_KEYS, t), dt)
    hr = pl.BlockSpec((PEER_HEADS, tp // LANES, PEER_KEYS, LANES), lambda i: (0, i, 0, 0))
    hr_shape = jax.ShapeDtypeStruct((PEER_HEADS, t // LANES, PEER_KEYS, LANES), F32)
    rows = (tp // LANES) * PEER_KEYS
    return pl.pallas_call(
        _peer_prep_kernel,
        grid=(t // tp,),
        in_specs=[pl.BlockSpec((tp, d), lambda i: (i, 0)), per_b, per_b, full(norm_w),
                  pl.BlockSpec(w_qt.shape, lambda i: (0, 0), pipeline_mode=pl.Buffered(1)),
                  full(keys)],
        out_specs=(pl.BlockSpec((d, tp), lambda i: (0, i)), hr, hr, hk, hk),
        out_shape=(jax.ShapeDtypeStruct((d, t), BF16), hr_shape, hr_shape,
                   hk_shape(BF16), hk_shape(BF16)),
        scratch_shapes=[pltpu.VMEM((w_qt.shape[0], tp), BF16), pltpu.VMEM((2, rows, LANES), F32),
                        pltpu.VMEM((4, rows, LANES), F32)],
        compiler_params=_cparams(1),
        name="peer_prep",
    )(x1, shift, scale, norm_w, w_qt, keys)


def _gelu_x2(x):
    k = math.sqrt(2.0 / math.pi)
    inner = x * (x * x * (0.044715 * k) + k)
    return x * jnp.tanh(inner) + x


def _peer_dense_kernel(n_eb, ht_ref, u_ref, vt_ref, c1_ref, e1_ref, code2_ref, e2_ref, x_ref,
                       gate_ref, fnw_ref, o_ref, acc_ref, act_a_ref, act_b_ref, wg_ref):
    s = pl.program_id(0)
    jv = lax.rem(jnp.maximum(s - 1, 0), n_eb)
    slot = lax.rem(s, 2)

    @pl.when(s == 0)
    def _():
        act_b_ref[...] = jnp.zeros_like(act_b_ref)

    @pl.when(jv == 0)
    def _():
        acc_ref[...] = jnp.zeros_like(acc_ref)

    tb = act_a_ref.shape[1]
    n_tiles = PEER_KEYS // BF16_ROWS

    def step(act_new, act_prev):
        per_dot = ACT_DOT_ROWS // PEER_KEYS
        for i1 in range(c1_ref.shape[2]):
            rows = slice(i1 * PEER_KEYS, (i1 + 1) * PEER_KEYS)
            if i1 % per_dot == 0:
                dot_rows = slice(i1 * PEER_KEYS, i1 * PEER_KEYS + ACT_DOT_ROWS)
                act_rows = _gelu_x2(_dot(u_ref[dot_rows, :], ht_ref[...])).astype(BF16)
            bcast = pl.ds(i1, BF16_ROWS, stride=0)
            for g0 in range(0, tb // LANES, GATE_LANE_GROUPS):
                lts = range(g0, g0 + GATE_LANE_GROUPS)
                ls = slice(g0 * LANES, (g0 + GATE_LANE_GROUPS) * LANES)
                w = [None] * n_tiles
                for hd in range(PEER_HEADS):
                    c_row = jnp.concatenate([c1_ref[hd, lt, bcast, :] for lt in lts],
                                            axis=1).astype(BF16)
                    e_row = jnp.concatenate([e1_ref[hd, lt, bcast, :] for lt in lts],
                                            axis=1).astype(BF16)
                    for it in range(n_tiles):
                        rs = slice(it * BF16_ROWS, (it + 1) * BF16_ROWS)
                        sel = code2_ref[hd, rs, ls] >= c_row
                        term = jnp.where(sel, e2_ref[hd, rs, ls] * e_row, jnp.zeros_like(e_row))
                        w[it] = term if w[it] is None else w[it] + term
                wg_ref[rows, ls] = jnp.concatenate(w, axis=0) * act_prev[rows, ls]
            if i1 % per_dot == per_dot - 1:
                act_new[dot_rows, :] = act_rows
        acc_ref[...] += _dot(vt_ref[...], wg_ref[...])

    pl.when(slot == 0)(lambda: step(act_a_ref, act_b_ref))
    pl.when(slot == 1)(lambda: step(act_b_ref, act_a_ref))

    @pl.when(jnp.logical_and(s > 0, jv == n_eb - 1))
    def _():
        x2 = x_ref[...] + gate_ref[0] * acc_ref[...].T
        o_ref[...] = _rms(x2, fnw_ref[...])


def _peer_dense_call(ht, u_b, vt_b, c1, e1, code2, e2, x1, gate, fnw, tb, eb, seq):
    d, t = ht.shape
    e = u_b.shape[0]
    per_seq = seq // tb
    n_i1 = eb // PEER_KEYS
    n_eb = e // eb
    total = (t // tb) * n_eb
    ti1 = lambda s: jnp.minimum(s, total - 1) // n_eb
    ej1 = lambda s: jnp.minimum(s, total - 1) % n_eb
    ti2 = lambda s: jnp.maximum(s - 1, 0) // n_eb
    ej2 = lambda s: jnp.maximum(s - 1, 0) % n_eb
    rows1 = pl.BlockSpec((PEER_HEADS, tb // LANES, n_i1, LANES), lambda s: (0, ti2(s), ej2(s), 0))
    rows2 = pl.BlockSpec((PEER_HEADS, PEER_KEYS, tb), lambda s: (0, 0, ti2(s)))
    return pl.pallas_call(
        functools.partial(_peer_dense_kernel, n_eb),
        grid=(total + 1,),
        in_specs=[pl.BlockSpec((d, tb), lambda s: (0, ti1(s))),
                  pl.BlockSpec((eb, d), lambda s: (ej1(s), 0)),
                  pl.BlockSpec((d, eb), lambda s: (0, ej2(s))),
                  rows1, rows1, rows2, rows2,
                  pl.BlockSpec((tb, d), lambda s: (ti2(s), 0), pipeline_mode=pl.Buffered(1)),
                  pl.BlockSpec((1, 1, d), lambda s: (ti2(s) // per_seq, 0, 0)),
                  pl.BlockSpec((1, d), lambda s: (0, 0))],
        out_specs=pl.BlockSpec((tb, d), lambda s: (ti2(s), 0)),
        out_shape=jax.ShapeDtypeStruct((t, d), F32),
        scratch_shapes=[pltpu.VMEM((d, tb), F32), pltpu.VMEM((eb, tb), BF16),
                        pltpu.VMEM((eb, tb), BF16), pltpu.VMEM((eb, tb), BF16)],
        compiler_params=_cparams(1),
        name="peer_dense",
    )(ht, u_b, vt_b, c1, e1, code2, e2, x1, gate, fnw)


def _tile_sizes(seq):
    tm = min(512, seq)
    return tm, SUBLANES * LANES, min(512, seq), 2048


def kernel(x, c, ctx, c_ctx, w_mod, b_mod, norm1_w, w_in, conv_w, conv_b, dt_bias, a_log, d_skip,
           ssd_norm_w, gmlp_norm_w, gmlp_ws, gmlp_bs, w_out, norm2_w, peer_wq, peer_keys, peer_u,
           peer_v, final_norm_w):
    b, seq, d = x.shape
    assert w_mod.shape[0] == 1, "single-layer block"
    assert seq % CHUNK == 0 and ctx.shape[1] % CHUNK == 0 and b + 1 <= SUBLANES
    tm, tp, tb, eb = _tile_sizes(seq)
    assert seq % tm == 0 and seq % tp == 0 and seq % tb == 0 and seq % SSD_STEP_ROWS == 0

    cc = jnp.zeros((SUBLANES, d), F32).at[:b].set(c).at[b].set(c_ctx)
    mod = _mod_call(cc, w_mod[0], b_mod[0][None, :]).reshape(SUBLANES, 6, d)
    mod_x = [mod[:b, k][:, None, :] for k in range(6)]
    shift_s, scale_s = mod[b, 0][None, :], mod[b, 1][None, :]

    nf = d // 4
    omega = (1.0 / (10000.0 ** (jnp.arange(nf, dtype=F32) / nf)))[None, :]
    pos_row, pos_col = _pos_call(seq, d, omega)

    w = w_in[0]
    o1, o2 = SSD_WIDTH, SSD_WIDTH + CONV_CH
    o3 = o2 + 2 * SSD_HEADS
    dt_pad = jnp.zeros((d, LANES - 2 * SSD_HEADS), F32)
    w_dt = jnp.concatenate([w[:, o2:o3], dt_pad], axis=1)
    w_main = jnp.concatenate([w[:, :o2], w[:, o3:], w_dt], axis=1).astype(BF16)
    w_xd = jnp.concatenate([w[:, o1:o2], w_dt], axis=1).astype(BF16)
    pad1 = jnp.zeros((1, LANES - 2 * SSD_HEADS), F32)
    dtb = jnp.concatenate([dt_bias[0].reshape(1, -1), pad1], axis=1)
    alog = jnp.concatenate([a_log[0].reshape(1, -1), pad1], axis=1)
    dskip = jnp.repeat(d_skip[0], SSD_HEAD_DIM)[None, :]
    cw, cb = conv_w[0], conv_b[0][None, :]
    n1 = norm1_w[0][None, :]

    st_f, st_b = _ctx_call(ctx, shift_s, scale_s, n1, w_xd, cw, cb, dtb, alog)

    gw = gmlp_ws[0].astype(BF16)
    gnw = gmlp_norm_w[0].reshape(1, GMLP_WIDTH)
    gb = jnp.repeat(gmlp_bs[0].T, GMLP_GROUP_DIM, axis=1)
    xp, z, xbc_raw, dt_raw, y_g = _inproj_call(x, pos_row, pos_col, mod_x[0], mod_x[1], n1, w_main, gw,
                                               gnw, gb, tm)

    y_f = _ssd_fwd_call(xbc_raw, dt_raw, cw, cb, dtb, alog, dskip, st_f)
    x1 = _ssd_bwd_call(xbc_raw, dt_raw, cw, cb, dtb, alog, st_b, y_f, z, y_g, xp, mod_x[2],
                       ssd_norm_w[0][None, :], w_out[0].astype(BF16))

    x1f = x1.reshape(b * seq, d)
    ht, c1, e1, code2, e2 = _peer_prep_call(
        x1f, mod_x[3], mod_x[4], norm2_w[0][None, :], peer_wq[0].astype(BF16).T,
        peer_keys[0].astype(BF16), tp, seq)
    out = _peer_dense_call(ht, peer_u[0].astype(BF16), peer_v[0].astype(BF16).T, c1, e1, code2, e2,
                           x1f, mod_x[5], final_norm_w[None, :], tb, eb, seq)
    return out.reshape(b, seq, d)
```

```python
import functools
import math

import jax
import jax.numpy as jnp
from jax import lax
from jax.experimental import pallas as pl
from jax.experimental.pallas import tpu as pltpu

F32 = jnp.float32
BF16 = jnp.bfloat16

GRID_W = 64
SSD_HEAD_DIM = 64
SSD_HEADS = 16
SSD_GROUPS = 2
HEADS_PER_GROUP = SSD_HEADS // SSD_GROUPS
SSD_STATE = 128
CHUNK = 128
SSD_WIDTH = SSD_HEADS * SSD_HEAD_DIM
GN = SSD_GROUPS * SSD_STATE
CONV_CH = SSD_WIDTH + 2 * GN
GMLP_GROUPS = 8
GMLP_GROUP_DIM = 128
GMLP_WIDTH = GMLP_GROUPS * GMLP_GROUP_DIM
PEER_KEYS = 128
PEER_HEADS = 8
PEER_HALF_DIM = 128
PEER_TOPK = 16
EPS = 1e-6

LANES = 128
SUBLANES = 8
BF16_ROWS = 16
VMEM_LIMIT = 56 * 1024 * 1024

GROUP_W = HEADS_PER_GROUP * SSD_HEAD_DIM
SSD_STEP_ROWS = 4 * CHUNK
GATE_LANE_GROUPS = 4
ACT_DOT_ROWS = 128


def _cparams(n_axes, flags=None):
    return pltpu.CompilerParams(
        dimension_semantics=("arbitrary",) * n_axes,
        vmem_limit_bytes=VMEM_LIMIT, flags=flags)


def _rms(x, w):
    return x * lax.rsqrt(jnp.mean(x * x, axis=-1, keepdims=True) + EPS) * w


def _silu(x):
    return x * jax.nn.sigmoid(x)


def _dot(a, b):
    return jnp.dot(a, b, preferred_element_type=F32)


def _dot_nt(a, b):
    return lax.dot_general(a, b, (((1,), (1,)), ((), ())), preferred_element_type=F32)


def _mod_kernel(c_ref, w_ref, b_ref, o_ref):
    a = _silu(c_ref[...]).astype(BF16)
    o_ref[...] = _dot(a, w_ref[...].astype(BF16)) + b_ref[...]


def _mod_call(cc, w_mod, b_mod):
    d, n = w_mod.shape
    bn = 1536
    return pl.pallas_call(
        _mod_kernel,
        grid=(n // bn,),
        in_specs=[pl.BlockSpec((SUBLANES, d), lambda j: (0, 0)),
                  pl.BlockSpec((d, bn), lambda j: (0, j)),
                  pl.BlockSpec((1, bn), lambda j: (0, j))],
        out_specs=pl.BlockSpec((SUBLANES, bn), lambda j: (0, j)),
        out_shape=jax.ShapeDtypeStruct((SUBLANES, n), F32),
        compiler_params=_cparams(1),
        name="mod",
    )(cc, w_mod, b_mod)


def _pos_kernel(omega_ref, row_ref, col_ref):
    nf = omega_ref.shape[1]
    om = omega_ref[...]
    for ref in (row_ref, col_ref):
        ang = lax.broadcasted_iota(jnp.int32, (ref.shape[0], nf), 0).astype(F32) * om
        ref[:, :nf] = jnp.sin(ang)
        ref[:, nf:] = jnp.cos(ang)


def _pos_call(seq, dim, omega):
    half = dim // 2
    return pl.pallas_call(
        _pos_kernel,
        out_shape=(jax.ShapeDtypeStruct((seq // GRID_W, half), F32),
                   jax.ShapeDtypeStruct((GRID_W, half), F32)),
        name="pos",
    )(omega)


def _conv_silu(x, prev_row, next_row, w, b):
    n = x.shape[0]
    rows = lax.broadcasted_iota(jnp.int32, x.shape, 0)
    x_prev = jnp.where(rows == 0, prev_row, pltpu.roll(x, 1, axis=0))
    x_next = jnp.where(rows == n - 1, next_row, pltpu.roll(x, n - 1, axis=0))
    y = b + x_prev * w[0:1, :] + x * w[1:2, :] + x_next * w[2:3, :]
    return _silu(y)


def _softplus(x):
    return jnp.maximum(x, 0.0) + jnp.log1p(jnp.exp(-jnp.abs(x)))


def _split3(x):
    hi = x.astype(BF16)
    r = x - hi.astype(F32)
    mid = r.astype(BF16)
    return hi, mid, (r - mid.astype(F32)).astype(BF16)


def _dot_exact_rhs(x, m):
    return sum(_dot(p, m) for p in _split3(x))


def _dot_exact_lhs(m, x):
    return sum(_dot(m, p) for p in _split3(x))


def _ssd_chunk(xbc, dt_all, a_all, expand, h_ref, reverse, want_y):
    q = xbc.shape[0]
    col0 = SSD_HEADS if reverse else 0
    xs = xbc[:, :SSD_WIDTH]
    d_a = dt_all * a_all
    ti = lax.broadcasted_iota(jnp.int32, (q, q), 0)
    si = lax.broadcasted_iota(jnp.int32, (q, q), 1)
    keep = (si >= ti) if reverse else (si <= ti)
    tri = jnp.where(keep, 1.0, 0.0).astype(BF16)
    acs = _dot_exact_lhs(tri, d_a)
    last = 0 if reverse else q - 1
    total = acs[last:last + 1, :]
    stacked = jnp.concatenate([jnp.exp(acs), dt_all * jnp.exp(total - acs)], axis=0)
    ex = _dot_exact_rhs(stacked, expand)
    ea_x = ex[:q]
    xd = (xs * ex[q:]).astype(BF16)
    xs_b = xs.astype(BF16)
    chunk_decay = ea_x[last:last + 1, :]
    acs_t = acs.T if want_y else None
    dt_t = dt_all.T if want_y else None
    lane = lax.broadcasted_iota(jnp.int32, (q, LANES), 1)
    ys = []
    for g in range(SSD_GROUPS):
        bm = xbc[:, SSD_WIDTH + g * SSD_STATE:SSD_WIDTH + (g + 1) * SSD_STATE]
        cm = xbc[:, SSD_WIDTH + GN + g * SSD_STATE:SSD_WIDTH + GN + (g + 1) * SSD_STATE]
        gsl = slice(g * GROUP_W, (g + 1) * GROUP_W)
        h_old = h_ref[g]
        if want_y:
            cm_b = cm.astype(BF16)
            cb = _dot_nt(cm_b, bm.astype(BF16))
            y_off = _dot(cm_b, h_old.astype(BF16)) * ea_x[:, gsl]
            pieces = []
            for pr in range(HEADS_PER_GROUP // 2):
                ms = []
                for r in (2 * pr, 2 * pr + 1):
                    j = col0 + g * HEADS_PER_GROUP + r
                    seg = acs[:, j:j + 1] - acs_t[j:j + 1, :]
                    lm = jnp.exp(jnp.where(keep, seg, -jnp.inf))
                    ms.append((cb * lm * dt_t[j:j + 1, :]).astype(BF16))
                c0 = g * GROUP_W + pr * LANES
                xp = xs_b[:, c0:c0 + LANES]
                zero = jnp.zeros_like(xp)
                rhs = jnp.concatenate([jnp.where(lane < SSD_HEAD_DIM, xp, zero),
                                       jnp.where(lane >= SSD_HEAD_DIM, xp, zero)], axis=0)
                pieces.append(_dot(jnp.concatenate(ms, axis=1), rhs))
            ys.append(jnp.concatenate(pieces, axis=1) + y_off)
        h_ref[g] = h_old * chunk_decay[:, gsl] + _dot(bm.T.astype(BF16), xd[:, gsl])
    return jnp.concatenate(ys, axis=1) if want_y else None


def _expand_matrix(reverse):
    rows = lax.broadcasted_iota(jnp.int32, (LANES, SSD_WIDTH), 0)
    cols = lax.broadcasted_iota(jnp.int32, (LANES, SSD_WIDTH), 1)
    head = cols // SSD_HEAD_DIM + (SSD_HEADS if reverse else 0)
    return jnp.where(rows == head, 1.0, 0.0).astype(BF16)


def _ctx_kernel(ctx_ref, shift_ref, scale_ref, nw_ref, w_ref, cw_ref, cb_ref, dtb_ref, alog_ref,
                stf_ref, stb_ref):
    h = _rms(ctx_ref[0], nw_ref[...]) * (1.0 + scale_ref[...]) + shift_ref[...]
    proj = _dot(h.astype(BF16), w_ref[...])
    n = proj.shape[0]
    zero_row = jnp.zeros((1, CONV_CH), F32)
    xbc = _conv_silu(proj[:, :CONV_CH], zero_row, zero_row, cw_ref[...], cb_ref[...])
    dt_all = _softplus(proj[:, CONV_CH:] + dtb_ref[...])
    a_all = -jnp.exp(alog_ref[...])
    stf_ref[...] = jnp.zeros_like(stf_ref)
    stb_ref[...] = jnp.zeros_like(stb_ref)
    nck = n // CHUNK
    ef, eb = _expand_matrix(False), _expand_matrix(True)
    for ci in range(nck):
        sl = slice(ci * CHUNK, (ci + 1) * CHUNK)
        _ssd_chunk(xbc[sl], dt_all[sl], a_all, ef, stf_ref.at[0], False, False)
    for ci in reversed(range(nck)):
        sl = slice(ci * CHUNK, (ci + 1) * CHUNK)
        _ssd_chunk(xbc[sl], dt_all[sl], a_all, eb, stb_ref.at[0], True, False)


def _ctx_call(ctx, shift, scale, norm_w, w_xd, conv_w, conv_b, dtb, alog):
    b, n, d = ctx.shape
    st_shape = jax.ShapeDtypeStruct((b, SSD_GROUPS, SSD_STATE, GROUP_W), F32)
    st_spec = pl.BlockSpec((1, SSD_GROUPS, SSD_STATE, GROUP_W), lambda i: (i, 0, 0, 0))
    full = lambda a: pl.BlockSpec(a.shape, lambda i: (0,) * a.ndim)
    return pl.pallas_call(
        _ctx_kernel,
        grid=(b,),
        in_specs=[pl.BlockSpec((1, n, d), lambda i: (i, 0, 0)),
                  full(shift), full(scale), full(norm_w), full(w_xd), full(conv_w), full(conv_b),
                  full(dtb), full(alog)],
        out_specs=(st_spec, st_spec),
        out_shape=(st_shape, st_shape),
        compiler_params=_cparams(1),
        name="ctx_state",
    )(ctx, shift, scale, norm_w, w_xd, conv_w, conv_b, dtb, alog)


def _inproj_kernel(x_ref, prow_ref, pcol_ref, shift_ref, scale_ref, nw_ref, w_ref, gw_ref, gnw_ref,
                   gb_ref, xp_ref, z_ref, xbc_ref, dt_ref, yg_ref):
    pcol = pcol_ref[...]
    pos = jnp.concatenate(
        [jnp.concatenate([jnp.broadcast_to(prow_ref[k:k + 1, :], pcol.shape), pcol], axis=1)
         for k in range(prow_ref.shape[0])], axis=0)
    xp = x_ref[0] + pos
    xp_ref[0] = xp
    h = _rms(xp, nw_ref[...]) * (1.0 + scale_ref[0]) + shift_ref[0]
    proj = _dot(h.astype(BF16), w_ref[...])
    z_ref[0] = proj[:, :SSD_WIDTH]
    xbc_ref[0] = proj[:, SSD_WIDTH:SSD_WIDTH + CONV_CH]
    o = SSD_WIDTH + CONV_CH
    u = jax.nn.gelu(proj[:, o:o + GMLP_WIDTH])
    v = jax.nn.gelu(proj[:, o + GMLP_WIDTH:o + 2 * GMLP_WIDTH])
    dt_ref[0] = proj[:, o + 2 * GMLP_WIDTH:]
    tm = u.shape[0]
    gnw = gnw_ref[...]
    gb = gb_ref[...]
    for g in range(GMLP_GROUPS):
        gs = slice(g * GMLP_GROUP_DIM, (g + 1) * GMLP_GROUP_DIM)
        vn = _rms(v[:, gs], gnw[:, gs]).astype(BF16)
        wg = gw_ref[g]
        for ci in range(tm // CHUNK):
            rs = slice(ci * CHUNK, (ci + 1) * CHUNK)
            mixed = _dot(wg, vn[rs]) + gb[:, gs]
            yg_ref[0, rs, gs] = (u[rs, gs] * mixed).astype(BF16)


def _inproj_call(x, pos_row, pos_col, shift, scale, norm_w, w_main, gw, gnw, gb, tm):
    b, seq, d = x.shape
    nt = seq // tm
    assert tm % (SUBLANES * GRID_W) == 0
    full = lambda a: pl.BlockSpec(a.shape, lambda i, j: (0,) * a.ndim)
    tok = lambda w: pl.BlockSpec((1, tm, w), lambda i, j: (i, j, 0))
    per_b = pl.BlockSpec((1, 1, d), lambda i, j: (i, 0, 0))
    shp = lambda w, dt: jax.ShapeDtypeStruct((b, seq, w), dt)
    return pl.pallas_call(
        _inproj_kernel,
        grid=(b, nt),
        in_specs=[tok(d), pl.BlockSpec((tm // GRID_W, d // 2), lambda i, j: (j, 0)), full(pos_col),
                  per_b, per_b, full(norm_w), full(w_main), full(gw), full(gnw), full(gb)],
        out_specs=(tok(d), tok(SSD_WIDTH), tok(CONV_CH), tok(LANES), tok(GMLP_WIDTH)),
        out_shape=(shp(d, F32), shp(SSD_WIDTH, F32), shp(CONV_CH, F32), shp(LANES, F32),
                   shp(GMLP_WIDTH, BF16)),
        compiler_params=_cparams(2),
        name="inproj",
    )(x, pos_row, pos_col, shift, scale, norm_w, w_main, gw, gnw, gb)


def _halo_rows(prev_ref, next_ref, c, nc):
    prev_row = jnp.where(c > 0, prev_ref[0, SUBLANES - 1:SUBLANES, :], 0.0)
    next_row = jnp.where(c < nc - 1, next_ref[0, 0:1, :], 0.0)
    return prev_row, next_row


def _ssd_fwd_kernel(xbc_ref, prev_ref, next_ref, dt_ref, cw_ref, cb_ref, dtb_ref, alog_ref,
                    dskip_ref, h0_ref, y_ref, h_ref):
    c = pl.program_id(1)
    nc = pl.num_programs(1)

    @pl.when(c == 0)
    def _():
        h_ref[...] = h0_ref[0]

    prev_row, next_row = _halo_rows(prev_ref, next_ref, c, nc)
    xbc = _conv_silu(xbc_ref[0], prev_row, next_row, cw_ref[...], cb_ref[...])
    dt_all = _softplus(dt_ref[0] + dtb_ref[...])
    a_all = -jnp.exp(alog_ref[...])
    expand = _expand_matrix(False)
    for ci in range(xbc.shape[0] // CHUNK):
        rs = slice(ci * CHUNK, (ci + 1) * CHUNK)
        y = _ssd_chunk(xbc[rs], dt_all[rs], a_all, expand, h_ref, False, True)
        y_ref[0, rs, :] = y + dskip_ref[...] * xbc[rs, :SSD_WIDTH]


def _ssd_bwd_kernel(xbc_ref, prev_ref, next_ref, dt_ref, cw_ref, cb_ref, dtb_ref, alog_ref,
                    h0_ref, yf_ref, z_ref, yg_ref, xp_ref, gate_ref, snw_ref, wo_ref,
                    o_ref, h_ref, ys_ref):
    cr = pl.program_id(1)
    nc = pl.num_programs(1)
    c = nc - 1 - cr

    @pl.when(cr == 0)
    def _():
        h_ref[...] = h0_ref[0]

    prev_row, next_row = _halo_rows(prev_ref, next_ref, c, nc)
    xbc = _conv_silu(xbc_ref[0], prev_row, next_row, cw_ref[...], cb_ref[...])
    dt_all = _softplus(dt_ref[0] + dtb_ref[...])
    a_all = -jnp.exp(alog_ref[...])
    expand = _expand_matrix(True)
    for ci in reversed(range(xbc.shape[0] // CHUNK)):
        rs = slice(ci * CHUNK, (ci + 1) * CHUNK)
        y = yf_ref[0, rs, :] + _ssd_chunk(xbc[rs], dt_all[rs], a_all, expand, h_ref, True, True)
        ys_ref[rs, :] = _rms(y * _silu(z_ref[0, rs, :]), snw_ref[...]).astype(BF16)
    out = _dot(ys_ref[...], wo_ref[:SSD_WIDTH, :]) + _dot(yg_ref[0], wo_ref[SSD_WIDTH:, :])
    o_ref[0] = xp_ref[0] + gate_ref[0] * out


def _ssd_specs(seq, reverse):
    rows = min(SSD_STEP_ROWS, seq)
    nc = seq // rows
    hb = rows // SUBLANES
    last_hb = seq // SUBLANES - 1
    cidx = (lambda j: nc - 1 - j) if reverse else (lambda j: j)
    cur = lambda w: pl.BlockSpec((1, rows, w), lambda i, j: (i, cidx(j), 0))
    prev = pl.BlockSpec((1, SUBLANES, CONV_CH),
                        lambda i, j: (i, jnp.maximum(cidx(j) * hb - 1, 0), 0))
    nxt = pl.BlockSpec((1, SUBLANES, CONV_CH),
                       lambda i, j: (i, jnp.minimum((cidx(j) + 1) * hb, last_hb), 0))
    return nc, cur, prev, nxt


def _ssd_fwd_call(xbc_raw, dt_raw, conv_w, conv_b, dtb, alog, dskip, st_f):
    b, seq, _ = xbc_raw.shape
    nc, cur, prev, nxt = _ssd_specs(seq, False)
    full = lambda a: pl.BlockSpec(a.shape, lambda i, j: (0,) * a.ndim)
    st = pl.BlockSpec((1, SSD_GROUPS, SSD_STATE, GROUP_W), lambda i, j: (i, 0, 0, 0))
    return pl.pallas_call(
        _ssd_fwd_kernel,
        grid=(b, nc),
        in_specs=[cur(CONV_CH), prev, nxt, cur(LANES), full(conv_w), full(conv_b), full(dtb),
                  full(alog), full(dskip), st],
        out_specs=cur(SSD_WIDTH),
        out_shape=jax.ShapeDtypeStruct((b, seq, SSD_WIDTH), F32),
        scratch_shapes=[pltpu.VMEM((SSD_GROUPS, SSD_STATE, GROUP_W), F32)],
        compiler_params=_cparams(2),
        name="ssd_fwd",
    )(xbc_raw, xbc_raw, xbc_raw, dt_raw, conv_w, conv_b, dtb, alog, dskip, st_f)


def _ssd_bwd_call(xbc_raw, dt_raw, conv_w, conv_b, dtb, alog, st_b, y_f, z, y_g, xp, gate, snw, w_out):
    b, seq, d = xp.shape
    nc, cur, prev, nxt = _ssd_specs(seq, True)
    full = lambda a: pl.BlockSpec(a.shape, lambda i, j: (0,) * a.ndim)
    st = pl.BlockSpec((1, SSD_GROUPS, SSD_STATE, GROUP_W), lambda i, j: (i, 0, 0, 0))
    per_b = pl.BlockSpec((1, 1, d), lambda i, j: (i, 0, 0))
    return pl.pallas_call(
        _ssd_bwd_kernel,
        grid=(b, nc),
        in_specs=[cur(CONV_CH), prev, nxt, cur(LANES), full(conv_w), full(conv_b), full(dtb),
                  full(alog), st, cur(SSD_WIDTH), cur(SSD_WIDTH), cur(GMLP_WIDTH), cur(d), per_b,
                  full(snw), full(w_out)],
        out_specs=cur(d),
        out_shape=jax.ShapeDtypeStruct((b, seq, d), F32),
        scratch_shapes=[pltpu.VMEM((SSD_GROUPS, SSD_STATE, GROUP_W), F32),
                        pltpu.VMEM((min(SSD_STEP_ROWS, seq), SSD_WIDTH), BF16)],
        compiler_params=_cparams(2),
        name="ssd_bwd",
    )(xbc_raw, xbc_raw, xbc_raw, dt_raw, conv_w, conv_b, dtb, alog, st_b, y_f, z, y_g, xp, gate,
      snw, w_out)


def _oem_sort_pairs(n):
    pairs = []
    p = 1
    while p < n:
        k = p
        while k >= 1:
            for j in range(k % p, n - k, 2 * k):
                for i in range(min(k, n - j - k)):
                    if (i + j) // (2 * p) == (i + j + k) // (2 * p):
                        pairs.append((i + j, i + j + k))
            k //= 2
        p *= 2
    return pairs


def _cmp_exchange(x, y):
    if x is None:
        return y, None
    if y is None:
        return x, None
    return jnp.maximum(x, y), jnp.minimum(x, y)


def _merge_top(xs, ys, n):
    xs = list(xs) + [None] * (n - len(xs))
    ys = list(ys) + [None] * (n - len(ys))
    out = [_cmp_exchange(xs[k], ys[n - 1 - k])[0] for k in range(n)]
    stride = n // 2
    while stride >= 1:
        for i in range(n):
            if i & stride == 0:
                out[i], out[i + stride] = _cmp_exchange(out[i], out[i + stride])
        stride //= 2
    return [v for v in out if v is not None]


def _top16_sorted(vals):
    groups = []
    for g in range(0, len(vals), PEER_TOPK):
        grp = list(vals[g:g + PEER_TOPK])
        for i, j in _oem_sort_pairs(PEER_TOPK):
            grp[i], grp[j] = _cmp_exchange(grp[i], grp[j])
        groups.append(grp)
    while len(groups) > 1:
        groups = [_merge_top(groups[i], groups[i + 1], PEER_TOPK) for i in range(0, len(groups), 2)]
    return groups[0]


def _peer_prep_kernel(x_ref, shift_ref, scale_ref, nw_ref, wqt_ref, keys_ref,
                      ht_ref, c1_ref, e1_ref, code2_ref, e2_ref, q_scr, s_scr, o_scr):
    h = _rms(x_ref[...], nw_ref[...]) * (1.0 + scale_ref[0]) + shift_ref[0]
    ht = h.T.astype(BF16)
    ht_ref[...] = ht
    tp = ht.shape[1]
    ntg = tp // LANES
    assert ntg == SUBLANES
    q_rows_per_dot = 4 * PEER_HALF_DIM
    for r0 in range(0, q_scr.shape[0], q_rows_per_dot):
        q_scr[r0:r0 + q_rows_per_dot, :] = _dot(wqt_ref[r0:r0 + q_rows_per_dot, :], ht).astype(BF16)

    def head(hd, carry):
        vals = []
        for k in range(2):
            q_rows = pl.ds(pl.multiple_of((hd * 2 + k) * PEER_HALF_DIM, PEER_HALF_DIM), PEER_HALF_DIM)
            st = _dot(keys_ref[k], q_scr[q_rows, :])
            for tg in range(ntg):
                s_scr[k, tg * PEER_KEYS:(tg + 1) * PEER_KEYS, :] = st[:, tg * LANES:(tg + 1) * LANES]
            vals.append([s_scr[k, pl.ds(key, ntg, stride=PEER_KEYS), :] for key in range(PEER_KEYS)])
        a = _top16_sorted(vals[0])
        b = _top16_sorted(vals[1])
        rows = [[a[i] + b[j] for j in range(PEER_TOPK // (i + 1))] for i in range(PEER_TOPK // 2)]
        col0 = [a[i] + b[0] for i in range(PEER_TOPK // 2, PEER_TOPK)]
        top = _merge_top(rows[0], _merge_top(rows[1], col0, PEER_TOPK), PEER_TOPK)
        rest = _merge_top(_merge_top(rows[2], rows[3], PEER_TOPK),
                          _merge_top(_merge_top(rows[4], rows[5], PEER_TOPK),
                                     _merge_top(rows[6], rows[7], PEER_TOPK), PEER_TOPK), PEER_TOPK)
        top = _merge_top(top, rest, PEER_TOPK)
        thr = top[-1]
        zsum = sum(jnp.exp(t - top[0]) for t in top[1:]) + 1.0
        half_inv_z = 0.5 / zsum
        none = float(PEER_TOPK + 1)
        c_rank = []
        for i in range(PEER_TOPK):
            row = rows[i] if i < PEER_TOPK // 2 else [col0[i - PEER_TOPK // 2]]
            c_rank.append(none - sum(jnp.where(v >= thr, 1.0, 0.0) for v in row))
        for kb in range(0, PEER_KEYS, SUBLANES):
            keys = range(kb, kb + SUBLANES)
            c1 = {key: jnp.full_like(thr, none) for key in keys}
            code2 = {key: jnp.zeros_like(thr) for key in keys}
            for r in reversed(range(PEER_TOPK)):
                for key in keys:
                    c1[key] = jnp.where(vals[0][key] >= a[r], c_rank[r], c1[key])
                    code2[key] = jnp.where(vals[1][key] >= b[r], float(PEER_TOPK - r), code2[key])
            for key in keys:
                sl = slice(key * ntg, (key + 1) * ntg)
                o_scr[0, sl, :] = c1[key]
                o_scr[1, sl, :] = jnp.exp(vals[0][key] - a[0]) * half_inv_z
                o_scr[2, sl, :] = code2[key]
                o_scr[3, sl, :] = jnp.exp(vals[1][key] - b[0])
        for tg in range(ntg):
            rs = pl.ds(tg, PEER_KEYS, stride=ntg)
            ls = slice(tg * LANES, (tg + 1) * LANES)
            c1_ref[hd, tg] = o_scr[0, rs, :]
            e1_ref[hd, tg] = o_scr[1, rs, :]
            code2_ref[hd, :, ls] = o_scr[2, rs, :].astype(BF16)
            e2_ref[hd, :, ls] = o_scr[3, rs, :].astype(BF16)
        return carry

    lax.fori_loop(0, PEER_HEADS, head, 0)


def _peer_prep_call(x1, shift, scale, norm_w, w_qt, keys, tp, seq):
    t, d = x1.shape
    per_seq = seq // tp
    full = lambda a: pl.BlockSpec(a.shape, lambda i: (0,) * a.ndim)
    per_b = pl.BlockSpec((1, 1, d), lambda i: (i // per_seq, 0, 0))
    hk = pl.BlockSpec((PEER_HEADS, PEER_KEYS, tp), lambda i: (0, 0, i))
    hk_shape = lambda dt: jax.ShapeDtypeStruct((PEER_HEADS, PEER_KEYS, t), dt)
    hr = pl.BlockSpec((PEER_HEADS, tp // LANES, PEER_KEYS, LANES), lambda i: (0, i, 0, 0))
    hr_shape = jax.ShapeDtypeStruct((PEER_HEADS, t // LANES, PEER_KEYS, LANES), F32)
    rows = (tp // LANES) * PEER_KEYS
    return pl.pallas_call(
        _peer_prep_kernel,
        grid=(t // tp,),
        in_specs=[pl.BlockSpec((tp, d), lambda i: (i, 0)), per_b, per_b, full(norm_w),
                  pl.BlockSpec(w_qt.shape, lambda i: (0, 0), pipeline_mode=pl.Buffered(1)),
                  full(keys)],
        out_specs=(pl.BlockSpec((d, tp), lambda i: (0, i)), hr, hr, hk, hk),
        out_shape=(jax.ShapeDtypeStruct((d, t), BF16), hr_shape, hr_shape,
                   hk_shape(BF16), hk_shape(BF16)),
        scratch_shapes=[pltpu.VMEM((w_qt.shape[0], tp), BF16), pltpu.VMEM((2, rows, LANES), F32),
                        pltpu.VMEM((4, rows, LANES), F32)],
        compiler_params=_cparams(1),
        name="peer_prep",
    )(x1, shift, scale, norm_w, w_qt, keys)


def _gelu_x2(x):
    k = math.sqrt(2.0 / math.pi)
    inner = x * (x * x * (0.044715 * k) + k)
    return x * jnp.tanh(inner) + x


def _peer_dense_kernel(n_eb, ht_ref, u_ref, vt_ref, c1_ref, e1_ref, code2_ref, e2_ref, x_ref,
                       gate_ref, fnw_ref, o_ref, acc_ref, act_a_ref, act_b_ref, wg_ref):
    s = pl.program_id(0)
    jv = lax.rem(jnp.maximum(s - 1, 0), n_eb)
    slot = lax.rem(s, 2)

    @pl.when(s == 0)
    def _():
        act_b_ref[...] = jnp.zeros_like(act_b_ref)

    @pl.when(jv == 0)
    def _():
        acc_ref[...] = jnp.zeros_like(acc_ref)

    tb = act_a_ref.shape[1]
    n_tiles = PEER_KEYS // BF16_ROWS

    def step(act_new, act_prev):
        per_dot = ACT_DOT_ROWS // PEER_KEYS
        for i1 in range(c1_ref.shape[2]):
            rows = slice(i1 * PEER_KEYS, (i1 + 1) * PEER_KEYS)
            if i1 % per_dot == 0:
                dot_rows = slice(i1 * PEER_KEYS, i1 * PEER_KEYS + ACT_DOT_ROWS)
                act_rows = _gelu_x2(_dot(u_ref[dot_rows, :], ht_ref[...])).astype(BF16)
            bcast = pl.ds(i1, BF16_ROWS, stride=0)
            for g0 in range(0, tb // LANES, GATE_LANE_GROUPS):
                lts = range(g0, g0 + GATE_LANE_GROUPS)
                ls = slice(g0 * LANES, (g0 + GATE_LANE_GROUPS) * LANES)
                w = [None] * n_tiles
                for hd in range(PEER_HEADS):
                    c_row = jnp.concatenate([c1_ref[hd, lt, bcast, :] for lt in lts],
                                            axis=1).astype(BF16)
                    e_row = jnp.concatenate([e1_ref[hd, lt, bcast, :] for lt in lts],
                                            axis=1).astype(BF16)
                    for it in range(n_tiles):
                        rs = slice(it * BF16_ROWS, (it + 1) * BF16_ROWS)
                        sel = code2_ref[hd, rs, ls] >= c_row
                        term = jnp.where(sel, e2_ref[hd, rs, ls] * e_row, jnp.zeros_like(e_row))
                        w[it] = term if w[it] is None else w[it] + term
                wg_ref[rows, ls] = jnp.concatenate(w, axis=0) * act_prev[rows, ls]
            if i1 % per_dot == per_dot - 1:
                act_new[dot_rows, :] = act_rows
        acc_ref[...] += _dot(vt_ref[0], wg_ref[...])

    pl.when(slot == 0)(lambda: step(act_a_ref, act_b_ref))
    pl.when(slot == 1)(lambda: step(act_b_ref, act_a_ref))

    @pl.when(jnp.logical_and(s > 0, jv == n_eb - 1))
    def _():
        x2 = x_ref[...] + gate_ref[0] * acc_ref[...].T
        o_ref[...] = _rms(x2, fnw_ref[...])


def _peer_dense_call(ht, u_b, vt_b, c1, e1, code2, e2, x1, gate, fnw, tb, eb, seq):
    d, t = ht.shape
    e = u_b.shape[0]
    per_seq = seq // tb
    n_i1 = eb // PEER_KEYS
    n_eb = e // eb
    total = (t // tb) * n_eb
    ti1 = lambda s: jnp.minimum(s, total - 1) // n_eb
    ej1 = lambda s: jnp.minimum(s, total - 1) % n_eb
    ti2 = lambda s: jnp.maximum(s - 1, 0) // n_eb
    ej2 = lambda s: jnp.maximum(s - 1, 0) % n_eb
    rows1 = pl.BlockSpec((PEER_HEADS, tb // LANES, n_i1, LANES), lambda s: (0, ti2(s), ej2(s), 0))
    rows2 = pl.BlockSpec((PEER_HEADS, PEER_KEYS, tb), lambda s: (0, 0, ti2(s)))
    return pl.pallas_call(
        functools.partial(_peer_dense_kernel, n_eb),
        grid=(total + 1,),
        in_specs=[pl.BlockSpec((d, tb), lambda s: (0, ti1(s))),
                  pl.BlockSpec((eb, d), lambda s: (ej1(s), 0)),
                  pl.BlockSpec((1, d, eb), lambda s: (ej2(s), 0, 0)),
                  rows1, rows1, rows2, rows2,
                  pl.BlockSpec((tb, d), lambda s: (ti2(s), 0), pipeline_mode=pl.Buffered(1)),
                  pl.BlockSpec((1, 1, d), lambda s: (ti2(s) // per_seq, 0, 0)),
                  pl.BlockSpec((1, d), lambda s: (0, 0))],
        out_specs=pl.BlockSpec((tb, d), lambda s: (ti2(s), 0)),
        out_shape=jax.ShapeDtypeStruct((t, d), F32),
        scratch_shapes=[pltpu.VMEM((d, tb), F32), pltpu.VMEM((eb, tb), BF16),
                        pltpu.VMEM((eb, tb), BF16), pltpu.VMEM((eb, tb), BF16)],
        compiler_params=_cparams(1),
        name="peer_dense",
    )(ht, u_b, vt_b, c1, e1, code2, e2, x1, gate, fnw)


def _tile_sizes(seq):
    tm = min(512, seq)
    return tm, SUBLANES * LANES, min(512, seq), 2048


def kernel(x, c, ctx, c_ctx, w_mod, b_mod, norm1_w, w_in, conv_w, conv_b, dt_bias, a_log, d_skip,
           ssd_norm_w, gmlp_norm_w, gmlp_ws, gmlp_bs, w_out, norm2_w, peer_wq, peer_keys, peer_u,
           peer_v, final_norm_w):
    b, seq, d = x.shape
    assert w_mod.shape[0] == 1, "single-layer block"
    assert seq % CHUNK == 0 and ctx.shape[1] % CHUNK == 0 and b + 1 <= SUBLANES
    tm, tp, tb, eb = _tile_sizes(seq)
    assert seq % tm == 0 and seq % tp == 0 and seq % tb == 0 and seq % SSD_STEP_ROWS == 0

    cc = jnp.zeros((SUBLANES, d), F32).at[:b].set(c).at[b].set(c_ctx)
    mod = _mod_call(cc, w_mod[0], b_mod[0][None, :]).reshape(SUBLANES, 6, d)
    mod_x = [mod[:b, k][:, None, :] for k in range(6)]
    shift_s, scale_s = mod[b, 0][None, :], mod[b, 1][None, :]

    nf = d // 4
    omega = (1.0 / (10000.0 ** (jnp.arange(nf, dtype=F32) / nf)))[None, :]
    pos_row, pos_col = _pos_call(seq, d, omega)

    w = w_in[0]
    o1, o2 = SSD_WIDTH, SSD_WIDTH + CONV_CH
    o3 = o2 + 2 * SSD_HEADS
    dt_pad = jnp.zeros((d, LANES - 2 * SSD_HEADS), F32)
    w_dt = jnp.concatenate([w[:, o2:o3], dt_pad], axis=1)
    w_main = jnp.concatenate([w[:, :o2], w[:, o3:], w_dt], axis=1).astype(BF16)
    w_xd = jnp.concatenate([w[:, o1:o2], w_dt], axis=1).astype(BF16)
    pad1 = jnp.zeros((1, LANES - 2 * SSD_HEADS), F32)
    dtb = jnp.concatenate([dt_bias[0].reshape(1, -1), pad1], axis=1)
    alog = jnp.concatenate([a_log[0].reshape(1, -1), pad1], axis=1)
    dskip = jnp.repeat(d_skip[0], SSD_HEAD_DIM)[None, :]
    cw, cb = conv_w[0], conv_b[0][None, :]
    n1 = norm1_w[0][None, :]

    st_f, st_b = _ctx_call(ctx, shift_s, scale_s, n1, w_xd, cw, cb, dtb, alog)

    gw = gmlp_ws[0].astype(BF16)
    gnw = gmlp_norm_w[0].reshape(1, GMLP_WIDTH)
    gb = jnp.repeat(gmlp_bs[0].T, GMLP_GROUP_DIM, axis=1)
    xp, z, xbc_raw, dt_raw, y_g = _inproj_call(x, pos_row, pos_col, mod_x[0], mod_x[1], n1, w_main, gw,
                                               gnw, gb, tm)

    y_f = _ssd_fwd_call(xbc_raw, dt_raw, cw, cb, dtb, alog, dskip, st_f)
    x1 = _ssd_bwd_call(xbc_raw, dt_raw, cw, cb, dtb, alog, st_b, y_f, z, y_g, xp, mod_x[2],
                       ssd_norm_w[0][None, :], w_out[0].astype(BF16))

    x1f = x1.reshape(b * seq, d)
    ht, c1, e1, code2, e2 = _peer_prep_call(
        x1f, mod_x[3], mod_x[4], norm2_w[0][None, :], peer_wq[0].astype(BF16).T,
        peer_keys[0].astype(BF16), tp, seq)
    vt_blocks = peer_v[0].astype(BF16).reshape(-1, eb, d).transpose(0, 2, 1)
    out = _peer_dense_call(ht, peer_u[0].astype(BF16), vt_blocks, c1, e1, code2, e2,
                           x1f, mod_x[5], final_norm_w[None, :], tb, eb, seq)
    return out.reshape(b, seq, d)
```

```python
import functools
import math

import jax
import jax.numpy as jnp
from jax import lax
from jax.experimental import pallas as pl
from jax.experimental.pallas import tpu as pltpu

F32 = jnp.float32
BF16 = jnp.bfloat16

GRID_W = 64
SSD_HEAD_DIM = 64
SSD_HEADS = 16
SSD_GROUPS = 2
HEADS_PER_GROUP = SSD_HEADS // SSD_GROUPS
SSD_STATE = 128
CHUNK = 128
SSD_WIDTH = SSD_HEADS * SSD_HEAD_DIM
GN = SSD_GROUPS * SSD_STATE
CONV_CH = SSD_WIDTH + 2 * GN
GMLP_GROUPS = 8
GMLP_GROUP_DIM = 128
GMLP_WIDTH = GMLP_GROUPS * GMLP_GROUP_DIM
PEER_KEYS = 128
PEER_HEADS = 8
PEER_HALF_DIM = 128
PEER_TOPK = 16
EPS = 1e-6

LANES = 128
SUBLANES = 8
BF16_ROWS = 16
VMEM_LIMIT = 56 * 1024 * 1024

GROUP_W = HEADS_PER_GROUP * SSD_HEAD_DIM
SSD_STEP_ROWS = 4 * CHUNK
GATE_LANE_GROUPS = 4
ACT_DOT_ROWS = 128


def _cparams(n_axes, flags=None):
    return pltpu.CompilerParams(
        dimension_semantics=("arbitrary",) * n_axes,
        vmem_limit_bytes=VMEM_LIMIT, flags=flags)


def _rms(x, w):
    return x * lax.rsqrt(jnp.mean(x * x, axis=-1, keepdims=True) + EPS) * w


def _silu(x):
    return x * jax.nn.sigmoid(x)


def _dot(a, b):
    return jnp.dot(a, b, preferred_element_type=F32)


def _dot_nt(a, b):
    return lax.dot_general(a, b, (((1,), (1,)), ((), ())), preferred_element_type=F32)


def _mod_kernel(c_ref, w_ref, b_ref, o_ref):
    a = _silu(c_ref[...]).astype(BF16)
    o_ref[...] = _dot(a, w_ref[...].astype(BF16)) + b_ref[...]


def _mod_call(cc, w_mod, b_mod):
    d, n = w_mod.shape
    bn = 1536
    return pl.pallas_call(
        _mod_kernel,
        grid=(n // bn,),
        in_specs=[pl.BlockSpec((SUBLANES, d), lambda j: (0, 0)),
                  pl.BlockSpec((d, bn), lambda j: (0, j)),
                  pl.BlockSpec((1, bn), lambda j: (0, j))],
        out_specs=pl.BlockSpec((SUBLANES, bn), lambda j: (0, j)),
        out_shape=jax.ShapeDtypeStruct((SUBLANES, n), F32),
        compiler_params=_cparams(1),
        name="mod",
    )(cc, w_mod, b_mod)


def _pos_kernel(omega_ref, row_ref, col_ref):
    nf = omega_ref.shape[1]
    om = omega_ref[...]
    for ref in (row_ref, col_ref):
        ang = lax.broadcasted_iota(jnp.int32, (ref.shape[0], nf), 0).astype(F32) * om
        ref[:, :nf] = jnp.sin(ang)
        ref[:, nf:] = jnp.cos(ang)


def _pos_call(seq, dim, omega):
    half = dim // 2
    return pl.pallas_call(
        _pos_kernel,
        out_shape=(jax.ShapeDtypeStruct((seq // GRID_W, half), F32),
                   jax.ShapeDtypeStruct((GRID_W, half), F32)),
        name="pos",
    )(omega)


def _conv_silu(x, prev_row, next_row, w, b):
    n = x.shape[0]
    rows = lax.broadcasted_iota(jnp.int32, x.shape, 0)
    x_prev = jnp.where(rows == 0, prev_row, pltpu.roll(x, 1, axis=0))
    x_next = jnp.where(rows == n - 1, next_row, pltpu.roll(x, n - 1, axis=0))
    y = b + x_prev * w[0:1, :] + x * w[1:2, :] + x_next * w[2:3, :]
    return _silu(y)


def _softplus(x):
    return jnp.maximum(x, 0.0) + jnp.log1p(jnp.exp(-jnp.abs(x)))


def _split3(x):
    hi = x.astype(BF16)
    r = x - hi.astype(F32)
    mid = r.astype(BF16)
    return hi, mid, (r - mid.astype(F32)).astype(BF16)


def _dot_exact_rhs(x, m):
    return sum(_dot(p, m) for p in _split3(x))


def _dot_exact_lhs(m, x):
    return sum(_dot(m, p) for p in _split3(x))


def _ssd_chunk(xbc, dt_all, a_all, expand, h_ref, reverse, want_y):
    q = xbc.shape[0]
    col0 = SSD_HEADS if reverse else 0
    xs = xbc[:, :SSD_WIDTH]
    d_a = dt_all * a_all
    ti = lax.broadcasted_iota(jnp.int32, (q, q), 0)
    si = lax.broadcasted_iota(jnp.int32, (q, q), 1)
    keep = (si >= ti) if reverse else (si <= ti)
    tri = jnp.where(keep, 1.0, 0.0).astype(BF16)
    acs = _dot_exact_lhs(tri, d_a)
    last = 0 if reverse else q - 1
    total = acs[last:last + 1, :]
    stacked = jnp.concatenate([jnp.exp(acs), dt_all * jnp.exp(total - acs)], axis=0)
    ex = _dot_exact_rhs(stacked, expand)
    ea_x = ex[:q]
    xd = (xs * ex[q:]).astype(BF16)
    xs_b = xs.astype(BF16)
    chunk_decay = ea_x[last:last + 1, :]
    acs_t = acs.T if want_y else None
    dt_t = dt_all.T if want_y else None
    lane = lax.broadcasted_iota(jnp.int32, (q, LANES), 1)
    ys = []
    for g in range(SSD_GROUPS):
        bm = xbc[:, SSD_WIDTH + g * SSD_STATE:SSD_WIDTH + (g + 1) * SSD_STATE]
        cm = xbc[:, SSD_WIDTH + GN + g * SSD_STATE:SSD_WIDTH + GN + (g + 1) * SSD_STATE]
        gsl = slice(g * GROUP_W, (g + 1) * GROUP_W)
        h_old = h_ref[g]
        if want_y:
            cm_b = cm.astype(BF16)
            cb = _dot_nt(cm_b, bm.astype(BF16))
            y_off = _dot(cm_b, h_old.astype(BF16)) * ea_x[:, gsl]
            pieces = []
            for pr in range(HEADS_PER_GROUP // 2):
                ms = []
                for r in (2 * pr, 2 * pr + 1):
                    j = col0 + g * HEADS_PER_GROUP + r
                    seg = acs[:, j:j + 1] - acs_t[j:j + 1, :]
                    lm = jnp.exp(jnp.where(keep, seg, -jnp.inf))
                    ms.append((cb * lm * dt_t[j:j + 1, :]).astype(BF16))
                c0 = g * GROUP_W + pr * LANES
                xp = xs_b[:, c0:c0 + LANES]
                zero = jnp.zeros_like(xp)
                rhs = jnp.concatenate([jnp.where(lane < SSD_HEAD_DIM, xp, zero),
                                       jnp.where(lane >= SSD_HEAD_DIM, xp, zero)], axis=0)
                pieces.append(_dot(jnp.concatenate(ms, axis=1), rhs))
            ys.append(jnp.concatenate(pieces, axis=1) + y_off)
        h_ref[g] = h_old * chunk_decay[:, gsl] + _dot(bm.T.astype(BF16), xd[:, gsl])
    return jnp.concatenate(ys, axis=1) if want_y else None


def _expand_matrix(reverse):
    rows = lax.broadcasted_iota(jnp.int32, (LANES, SSD_WIDTH), 0)
    cols = lax.broadcasted_iota(jnp.int32, (LANES, SSD_WIDTH), 1)
    head = cols // SSD_HEAD_DIM + (SSD_HEADS if reverse else 0)
    return jnp.where(rows == head, 1.0, 0.0).astype(BF16)


def _ctx_kernel(ctx_ref, shift_ref, scale_ref, nw_ref, w_ref, cw_ref, cb_ref, dtb_ref, alog_ref,
                stf_ref, stb_ref):
    h = _rms(ctx_ref[0], nw_ref[...]) * (1.0 + scale_ref[...]) + shift_ref[...]
    proj = _dot(h.astype(BF16), w_ref[...])
    n = proj.shape[0]
    zero_row = jnp.zeros((1, CONV_CH), F32)
    xbc = _conv_silu(proj[:, :CONV_CH], zero_row, zero_row, cw_ref[...], cb_ref[...])
    dt_all = _softplus(proj[:, CONV_CH:] + dtb_ref[...])
    a_all = -jnp.exp(alog_ref[...])
    stf_ref[...] = jnp.zeros_like(stf_ref)
    stb_ref[...] = jnp.zeros_like(stb_ref)
    nck = n // CHUNK
    ef, eb = _expand_matrix(False), _expand_matrix(True)
    for ci in range(nck):
        sl = slice(ci * CHUNK, (ci + 1) * CHUNK)
        _ssd_chunk(xbc[sl], dt_all[sl], a_all, ef, stf_ref.at[0], False, False)
    for ci in reversed(range(nck)):
        sl = slice(ci * CHUNK, (ci + 1) * CHUNK)
        _ssd_chunk(xbc[sl], dt_all[sl], a_all, eb, stb_ref.at[0], True, False)


def _ctx_call(ctx, shift, scale, norm_w, w_xd, conv_w, conv_b, dtb, alog):
    b, n, d = ctx.shape
    st_shape = jax.ShapeDtypeStruct((b, SSD_GROUPS, SSD_STATE, GROUP_W), F32)
    st_spec = pl.BlockSpec((1, SSD_GROUPS, SSD_STATE, GROUP_W), lambda i: (i, 0, 0, 0))
    full = lambda a: pl.BlockSpec(a.shape, lambda i: (0,) * a.ndim)
    return pl.pallas_call(
        _ctx_kernel,
        grid=(b,),
        in_specs=[pl.BlockSpec((1, n, d), lambda i: (i, 0, 0)),
                  full(shift), full(scale), full(norm_w), full(w_xd), full(conv_w), full(conv_b),
                  full(dtb), full(alog)],
        out_specs=(st_spec, st_spec),
        out_shape=(st_shape, st_shape),
        compiler_params=_cparams(1),
        name="ctx_state",
    )(ctx, shift, scale, norm_w, w_xd, conv_w, conv_b, dtb, alog)


def _inproj_kernel(x_ref, prow_ref, pcol_ref, shift_ref, scale_ref, nw_ref, w_ref, gw_ref, gnw_ref,
                   gb_ref, xp_ref, z_ref, xbc_ref, dt_ref, yg_ref):
    pcol = pcol_ref[...]
    pos = jnp.concatenate(
        [jnp.concatenate([jnp.broadcast_to(prow_ref[k:k + 1, :], pcol.shape), pcol], axis=1)
         for k in range(prow_ref.shape[0])], axis=0)
    xp = x_ref[0] + pos
    xp_ref[0] = xp
    h = _rms(xp, nw_ref[...]) * (1.0 + scale_ref[0]) + shift_ref[0]
    proj = _dot(h.astype(BF16), w_ref[...])
    z_ref[0] = proj[:, :SSD_WIDTH]
    xbc_ref[0] = proj[:, SSD_WIDTH:SSD_WIDTH + CONV_CH]
    o = SSD_WIDTH + CONV_CH
    u = jax.nn.gelu(proj[:, o:o + GMLP_WIDTH])
    v = jax.nn.gelu(proj[:, o + GMLP_WIDTH:o + 2 * GMLP_WIDTH])
    dt_ref[0] = proj[:, o + 2 * GMLP_WIDTH:]
    tm = u.shape[0]
    gnw = gnw_ref[...]
    gb = gb_ref[...]
    for g in range(GMLP_GROUPS):
        gs = slice(g * GMLP_GROUP_DIM, (g + 1) * GMLP_GROUP_DIM)
        vn = _rms(v[:, gs], gnw[:, gs]).astype(BF16)
        wg = gw_ref[g]
        for ci in range(tm // CHUNK):
            rs = slice(ci * CHUNK, (ci + 1) * CHUNK)
            mixed = _dot(wg, vn[rs]) + gb[:, gs]
            yg_ref[0, rs, gs] = (u[rs, gs] * mixed).astype(BF16)


def _inproj_call(x, pos_row, pos_col, shift, scale, norm_w, w_main, gw, gnw, gb, tm):
    b, seq, d = x.shape
    nt = seq // tm
    assert tm % (SUBLANES * GRID_W) == 0
    full = lambda a: pl.BlockSpec(a.shape, lambda i, j: (0,) * a.ndim)
    tok = lambda w: pl.BlockSpec((1, tm, w), lambda i, j: (i, j, 0))
    per_b = pl.BlockSpec((1, 1, d), lambda i, j: (i, 0, 0))
    shp = lambda w, dt: jax.ShapeDtypeStruct((b, seq, w), dt)
    return pl.pallas_call(
        _inproj_kernel,
        grid=(b, nt),
        in_specs=[tok(d), pl.BlockSpec((tm // GRID_W, d // 2), lambda i, j: (j, 0)), full(pos_col),
                  per_b, per_b, full(norm_w), full(w_main), full(gw), full(gnw), full(gb)],
        out_specs=(tok(d), tok(SSD_WIDTH), tok(CONV_CH), tok(LANES), tok(GMLP_WIDTH)),
        out_shape=(shp(d, F32), shp(SSD_WIDTH, F32), shp(CONV_CH, F32), shp(LANES, F32),
                   shp(GMLP_WIDTH, BF16)),
        compiler_params=_cparams(2),
        name="inproj",
    )(x, pos_row, pos_col, shift, scale, norm_w, w_main, gw, gnw, gb)


def _halo_rows(prev_ref, next_ref, c, nc):
    prev_row = jnp.where(c > 0, prev_ref[0, SUBLANES - 1:SUBLANES, :], 0.0)
    next_row = jnp.where(c < nc - 1, next_ref[0, 0:1, :], 0.0)
    return prev_row, next_row


def _ssd_fwd_kernel(xbc_ref, prev_ref, next_ref, dt_ref, cw_ref, cb_ref, dtb_ref, alog_ref,
                    dskip_ref, h0_ref, y_ref, h_ref):
    c = pl.program_id(1)
    nc = pl.num_programs(1)

    @pl.when(c == 0)
    def _():
        h_ref[...] = h0_ref[0]

    prev_row, next_row = _halo_rows(prev_ref, next_ref, c, nc)
    xbc = _conv_silu(xbc_ref[0], prev_row, next_row, cw_ref[...], cb_ref[...])
    dt_all = _softplus(dt_ref[0] + dtb_ref[...])
    a_all = -jnp.exp(alog_ref[...])
    expand = _expand_matrix(False)
    for ci in range(xbc.shape[0] // CHUNK):
        rs = slice(ci * CHUNK, (ci + 1) * CHUNK)
        y = _ssd_chunk(xbc[rs], dt_all[rs], a_all, expand, h_ref, False, True)
        y_ref[0, rs, :] = y + dskip_ref[...] * xbc[rs, :SSD_WIDTH]


def _ssd_bwd_kernel(xbc_ref, prev_ref, next_ref, dt_ref, cw_ref, cb_ref, dtb_ref, alog_ref,
                    h0_ref, yf_ref, z_ref, yg_ref, xp_ref, gate_ref, snw_ref, wo_ref,
                    o_ref, h_ref, ys_ref):
    cr = pl.program_id(1)
    nc = pl.num_programs(1)
    c = nc - 1 - cr

    @pl.when(cr == 0)
    def _():
        h_ref[...] = h0_ref[0]

    prev_row, next_row = _halo_rows(prev_ref, next_ref, c, nc)
    xbc = _conv_silu(xbc_ref[0], prev_row, next_row, cw_ref[...], cb_ref[...])
    dt_all = _softplus(dt_ref[0] + dtb_ref[...])
    a_all = -jnp.exp(alog_ref[...])
    expand = _expand_matrix(True)
    for ci in reversed(range(xbc.shape[0] // CHUNK)):
        rs = slice(ci * CHUNK, (ci + 1) * CHUNK)
        y = yf_ref[0, rs, :] + _ssd_chunk(xbc[rs], dt_all[rs], a_all, expand, h_ref, True, True)
        ys_ref[rs, :] = _rms(y * _silu(z_ref[0, rs, :]), snw_ref[...]).astype(BF16)
    out = _dot(ys_ref[...], wo_ref[:SSD_WIDTH, :]) + _dot(yg_ref[0], wo_ref[SSD_WIDTH:, :])
    o_ref[0] = xp_ref[0] + gate_ref[0] * out


def _ssd_specs(seq, reverse):
    rows = min(SSD_STEP_ROWS, seq)
    nc = seq // rows
    hb = rows // SUBLANES
    last_hb = seq // SUBLANES - 1
    cidx = (lambda j: nc - 1 - j) if reverse else (lambda j: j)
    cur = lambda w: pl.BlockSpec((1, rows, w), lambda i, j: (i, cidx(j), 0))
    prev = pl.BlockSpec((1, SUBLANES, CONV_CH),
                        lambda i, j: (i, jnp.maximum(cidx(j) * hb - 1, 0), 0))
    nxt = pl.BlockSpec((1, SUBLANES, CONV_CH),
                       lambda i, j: (i, jnp.minimum((cidx(j) + 1) * hb, last_hb), 0))
    return nc, cur, prev, nxt


def _ssd_fwd_call(xbc_raw, dt_raw, conv_w, conv_b, dtb, alog, dskip, st_f):
    b, seq, _ = xbc_raw.shape
    nc, cur, prev, nxt = _ssd_specs(seq, False)
    full = lambda a: pl.BlockSpec(a.shape, lambda i, j: (0,) * a.ndim)
    st = pl.BlockSpec((1, SSD_GROUPS, SSD_STATE, GROUP_W), lambda i, j: (i, 0, 0, 0))
    return pl.pallas_call(
        _ssd_fwd_kernel,
        grid=(b, nc),
        in_specs=[cur(CONV_CH), prev, nxt, cur(LANES), full(conv_w), full(conv_b), full(dtb),
                  full(alog), full(dskip), st],
        out_specs=cur(SSD_WIDTH),
        out_shape=jax.ShapeDtypeStruct((b, seq, SSD_WIDTH), F32),
        scratch_shapes=[pltpu.VMEM((SSD_GROUPS, SSD_STATE, GROUP_W), F32)],
        compiler_params=_cparams(2),
        name="ssd_fwd",
    )(xbc_raw, xbc_raw, xbc_raw, dt_raw, conv_w, conv_b, dtb, alog, dskip, st_f)


def _ssd_bwd_call(xbc_raw, dt_raw, conv_w, conv_b, dtb, alog, st_b, y_f, z, y_g, xp, gate, snw, w_out):
    b, seq, d = xp.shape
    nc, cur, prev, nxt = _ssd_specs(seq, True)
    full = lambda a: pl.BlockSpec(a.shape, lambda i, j: (0,) * a.ndim)
    st = pl.BlockSpec((1, SSD_GROUPS, SSD_STATE, GROUP_W), lambda i, j: (i, 0, 0, 0))
    per_b = pl.BlockSpec((1, 1, d), lambda i, j: (i, 0, 0))
    return pl.pallas_call(
        _ssd_bwd_kernel,
        grid=(b, nc),
        in_specs=[cur(CONV_CH), prev, nxt, cur(LANES), full(conv_w), full(conv_b), full(dtb),
                  full(alog), st, cur(SSD_WIDTH), cur(SSD_WIDTH), cur(GMLP_WIDTH), cur(d), per_b,
                  full(snw), full(w_out)],
        out_specs=cur(d),
        out_shape=jax.ShapeDtypeStruct((b, seq, d), F32),
        scratch_shapes=[pltpu.VMEM((SSD_GROUPS, SSD_STATE, GROUP_W), F32),
                        pltpu.VMEM((min(SSD_STEP_ROWS, seq), SSD_WIDTH), BF16)],
        compiler_params=_cparams(2),
        name="ssd_bwd",
    )(xbc_raw, xbc_raw, xbc_raw, dt_raw, conv_w, conv_b, dtb, alog, st_b, y_f, z, y_g, xp, gate,
      snw, w_out)


def _oem_sort_pairs(n):
    pairs = []
    p = 1
    while p < n:
        k = p
        while k >= 1:
            for j in range(k % p, n - k, 2 * k):
                for i in range(min(k, n - j - k)):
                    if (i + j) // (2 * p) == (i + j + k) // (2 * p):
                        pairs.append((i + j, i + j + k))
            k //= 2
        p *= 2
    return pairs


def _cmp_exchange(x, y):
    if x is None:
        return y, None
    if y is None:
        return x, None
    return jnp.maximum(x, y), jnp.minimum(x, y)


def _merge_top(xs, ys, n):
    xs = list(xs) + [None] * (n - len(xs))
    ys = list(ys) + [None] * (n - len(ys))
    out = [_cmp_exchange(xs[k], ys[n - 1 - k])[0] for k in range(n)]
    stride = n // 2
    while stride >= 1:
        for i in range(n):
            if i & stride == 0:
                out[i], out[i + stride] = _cmp_exchange(out[i], out[i + stride])
        stride //= 2
    return [v for v in out if v is not None]


def _top16_sorted(vals):
    groups = []
    for g in range(0, len(vals), PEER_TOPK):
        grp = list(vals[g:g + PEER_TOPK])
        for i, j in _oem_sort_pairs(PEER_TOPK):
            grp[i], grp[j] = _cmp_exchange(grp[i], grp[j])
        groups.append(grp)
    while len(groups) > 1:
        groups = [_merge_top(groups[i], groups[i + 1], PEER_TOPK) for i in range(0, len(groups), 2)]
    return groups[0]


def _peer_prep_kernel(x_ref, shift_ref, scale_ref, nw_ref, wqt_ref, keys_ref,
                      ht_ref, c1_ref, e1_ref, code2_ref, e2_ref, q_scr, s_scr, o_scr):
    h = _rms(x_ref[...], nw_ref[...]) * (1.0 + scale_ref[0]) + shift_ref[0]
    ht = h.T.astype(BF16)
    ht_ref[...] = ht
    tp = ht.shape[1]
    ntg = tp // LANES
    assert ntg == SUBLANES
    q_rows_per_dot = 4 * PEER_HALF_DIM
    for r0 in range(0, q_scr.shape[0], q_rows_per_dot):
        q_scr[r0:r0 + q_rows_per_dot, :] = _dot(wqt_ref[r0:r0 + q_rows_per_dot, :], ht).astype(BF16)

    def head(hd, carry):
        vals = []
        for k in range(2):
            q_rows = pl.ds(pl.multiple_of((hd * 2 + k) * PEER_HALF_DIM, PEER_HALF_DIM), PEER_HALF_DIM)
            st = _dot(keys_ref[k], q_scr[q_rows, :])
            for tg in range(ntg):
                s_scr[k, tg * PEER_KEYS:(tg + 1) * PEER_KEYS, :] = st[:, tg * LANES:(tg + 1) * LANES]
            vals.append([s_scr[k, pl.ds(key, ntg, stride=PEER_KEYS), :] for key in range(PEER_KEYS)])
        a = _top16_sorted(vals[0])
        b = _top16_sorted(vals[1])
        rows = [[a[i] + b[j] for j in range(PEER_TOPK // (i + 1))] for i in range(PEER_TOPK // 2)]
        col0 = [a[i] + b[0] for i in range(PEER_TOPK // 2, PEER_TOPK)]
        top = _merge_top(rows[0], _merge_top(rows[1], col0, PEER_TOPK), PEER_TOPK)
        rest = _merge_top(_merge_top(rows[2], rows[3], PEER_TOPK),
                          _merge_top(_merge_top(rows[4], rows[5], PEER_TOPK),
                                     _merge_top(rows[6], rows[7], PEER_TOPK), PEER_TOPK), PEER_TOPK)
        top = _merge_top(top, rest, PEER_TOPK)
        thr = top[-1]
        zsum = sum(jnp.exp(t - top[0]) for t in top[1:]) + 1.0
        half_inv_z = 0.5 / zsum
        none = float(PEER_TOPK + 1)
        grid_rows = rows + [[v] for v in col0]
        above = [sum(jnp.where(v > thr, 1.0, 0.0) for v in row) for row in grid_rows]
        tied = [sum(jnp.where(v == thr, 1.0, 0.0) for v in row) for row in grid_rows]
        left = float(PEER_TOPK) - sum(above)
        c_rank = []
        for i in range(PEER_TOPK):
            take = jnp.minimum(tied[i], left)
            left = left - take
            c_rank.append(none - above[i] - take)
        for kb in range(0, PEER_KEYS, SUBLANES):
            keys = range(kb, kb + SUBLANES)
            c1 = {key: jnp.full_like(thr, none) for key in keys}
            code2 = {key: jnp.zeros_like(thr) for key in keys}
            for r in reversed(range(PEER_TOPK)):
                for key in keys:
                    c1[key] = jnp.where(vals[0][key] >= a[r], c_rank[r], c1[key])
                    code2[key] = jnp.where(vals[1][key] >= b[r], float(PEER_TOPK - r), code2[key])
            for key in keys:
                sl = slice(key * ntg, (key + 1) * ntg)
                o_scr[0, sl, :] = c1[key]
                o_scr[1, sl, :] = jnp.exp(vals[0][key] - a[0]) * half_inv_z
                o_scr[2, sl, :] = code2[key]
                o_scr[3, sl, :] = jnp.exp(vals[1][key] - b[0])
        for tg in range(ntg):
            rs = pl.ds(tg, PEER_KEYS, stride=ntg)
            ls = slice(tg * LANES, (tg + 1) * LANES)
            c1_ref[hd, tg] = o_scr[0, rs, :]
            e1_ref[hd, tg] = o_scr[1, rs, :]
            code2_ref[hd, :, ls] = o_scr[2, rs, :].astype(BF16)
            e2_ref[hd, :, ls] = o_scr[3, rs, :].astype(BF16)
        return carry

    lax.fori_loop(0, PEER_HEADS, head, 0)


def _peer_prep_call(x1, shift, scale, norm_w, w_qt, keys, tp, seq):
    t, d = x1.shape
    per_seq = seq // tp
    full = lambda a: pl.BlockSpec(a.shape, lambda i: (0,) * a.ndim)
    per_b = pl.BlockSpec((1, 1, d), lambda i: (i // per_seq, 0, 0))
    hk = pl.BlockSpec((PEER_HEADS, PEER_KEYS, tp), lambda i: (0, 0, i))
    hk_shape = lambda dt: jax.ShapeDtypeStruct((PEER_HEADS, PEER_KEYS, t), dt)
    hr = pl.BlockSpec((PEER_HEADS, tp // LANES, PEER_KEYS, LANES), lambda i: (0, i, 0, 0))
    hr_shape = jax.ShapeDtypeStruct((PEER_HEADS, t // LANES, PEER_KEYS, LANES), F32)
    rows = (tp // LANES) * PEER_KEYS
    return pl.pallas_call(
        _peer_prep_kernel,
        grid=(t // tp,),
        in_specs=[pl.BlockSpec((tp, d), lambda i: (i, 0)), per_b, per_b, full(norm_w),
                  pl.BlockSpec(w_qt.shape, lambda i: (0, 0), pipeline_mode=pl.Buffered(1)),
                  full(keys)],
        out_specs=(pl.BlockSpec((d, tp), lambda i: (0, i)), hr, hr, hk, hk),
        out_shape=(jax.ShapeDtypeStruct((d, t), BF16), hr_shape, hr_shape,
                   hk_shape(BF16), hk_shape(BF16)),
        scratch_shapes=[pltpu.VMEM((w_qt.shape[0], tp), BF16), pltpu.VMEM((2, rows, LANES), F32),
                        pltpu.VMEM((4, rows, LANES), F32)],
        compiler_params=_cparams(1),
        name="peer_prep",
    )(x1, shift, scale, norm_w, w_qt, keys)


def _gelu_x2(x):
    k = math.sqrt(2.0 / math.pi)
    inner = x * (x * x * (0.044715 * k) + k)
    return x * jnp.tanh(inner) + x


def _peer_dense_kernel(n_eb, ht_ref, u_ref, vt_ref, c1_ref, e1_ref, code2_ref, e2_ref, x_ref,
                       gate_ref, fnw_ref, o_ref, acc_ref, act_a_ref, act_b_ref, wg_ref):
    s = pl.program_id(0)
    jv = lax.rem(jnp.maximum(s - 1, 0), n_eb)
    slot = lax.rem(s, 2)

    @pl.when(s == 0)
    def _():
        act_b_ref[...] = jnp.zeros_like(act_b_ref)

    @pl.when(jv == 0)
    def _():
        acc_ref[...] = jnp.zeros_like(acc_ref)

    tb = act_a_ref.shape[1]
    n_tiles = PEER_KEYS // BF16_ROWS

    def step(act_new, act_prev):
        per_dot = ACT_DOT_ROWS // PEER_KEYS
        for i1 in range(c1_ref.shape[2]):
            rows = slice(i1 * PEER_KEYS, (i1 + 1) * PEER_KEYS)
            if i1 % per_dot == 0:
                dot_rows = slice(i1 * PEER_KEYS, i1 * PEER_KEYS + ACT_DOT_ROWS)
                act_rows = _gelu_x2(_dot(u_ref[dot_rows, :], ht_ref[...])).astype(BF16)
            bcast = pl.ds(i1, BF16_ROWS, stride=0)
            for g0 in range(0, tb // LANES, GATE_LANE_GROUPS):
                lts = range(g0, g0 + GATE_LANE_GROUPS)
                ls = slice(g0 * LANES, (g0 + GATE_LANE_GROUPS) * LANES)
                w = [None] * n_tiles
                for hd in range(PEER_HEADS):
                    c_row = jnp.concatenate([c1_ref[hd, lt, bcast, :] for lt in lts],
                                            axis=1).astype(BF16)
                    e_row = jnp.concatenate([e1_ref[hd, lt, bcast, :] for lt in lts],
                                            axis=1).astype(BF16)
                    for it in range(n_tiles):
                        rs = slice(it * BF16_ROWS, (it + 1) * BF16_ROWS)
                        sel = code2_ref[hd, rs, ls] >= c_row
                        term = jnp.where(sel, e2_ref[hd, rs, ls] * e_row, jnp.zeros_like(e_row))
                        w[it] = term if w[it] is None else w[it] + term
                wg_ref[rows, ls] = jnp.concatenate(w, axis=0) * act_prev[rows, ls]
            if i1 % per_dot == per_dot - 1:
                act_new[dot_rows, :] = act_rows
        acc_ref[...] += _dot(vt_ref[0], wg_ref[...])

    pl.when(slot == 0)(lambda: step(act_a_ref, act_b_ref))
    pl.when(slot == 1)(lambda: step(act_b_ref, act_a_ref))

    @pl.when(jnp.logical_and(s > 0, jv == n_eb - 1))
    def _():
        x2 = x_ref[...] + gate_ref[0] * acc_ref[...].T
        o_ref[...] = _rms(x2, fnw_ref[...])


def _peer_dense_call(ht, u_b, vt_b, c1, e1, code2, e2, x1, gate, fnw, tb, eb, seq):
    d, t = ht.shape
    e = u_b.shape[0]
    per_seq = seq // tb
    n_i1 = eb // PEER_KEYS
    n_eb = e // eb
    total = (t // tb) * n_eb
    ti1 = lambda s: jnp.minimum(s, total - 1) // n_eb
    ej1 = lambda s: jnp.minimum(s, total - 1) % n_eb
    ti2 = lambda s: jnp.maximum(s - 1, 0) // n_eb
    ej2 = lambda s: jnp.maximum(s - 1, 0) % n_eb
    rows1 = pl.BlockSpec((PEER_HEADS, tb // LANES, n_i1, LANES), lambda s: (0, ti2(s), ej2(s), 0))
    rows2 = pl.BlockSpec((PEER_HEADS, PEER_KEYS, tb), lambda s: (0, 0, ti2(s)))
    return pl.pallas_call(
        functools.partial(_peer_dense_kernel, n_eb),
        grid=(total + 1,),
        in_specs=[pl.BlockSpec((d, tb), lambda s: (0, ti1(s))),
                  pl.BlockSpec((eb, d), lambda s: (ej1(s), 0)),
                  pl.BlockSpec((1, d, eb), lambda s: (ej2(s), 0, 0)),
                  rows1, rows1, rows2, rows2,
                  pl.BlockSpec((tb, d), lambda s: (ti2(s), 0), pipeline_mode=pl.Buffered(1)),
                  pl.BlockSpec((1, 1, d), lambda s: (ti2(s) // per_seq, 0, 0)),
                  pl.BlockSpec((1, d), lambda s: (0, 0))],
        out_specs=pl.BlockSpec((tb, d), lambda s: (ti2(s), 0)),
        out_shape=jax.ShapeDtypeStruct((t, d), F32),
        scratch_shapes=[pltpu.VMEM((d, tb), F32), pltpu.VMEM((eb, tb), BF16),
                        pltpu.VMEM((eb, tb), BF16), pltpu.VMEM((eb, tb), BF16)],
        compiler_params=_cparams(1),
        name="peer_dense",
    )(ht, u_b, vt_b, c1, e1, code2, e2, x1, gate, fnw)


def _tile_sizes(seq):
    tm = min(512, seq)
    return tm, SUBLANES * LANES, min(512, seq), 2048


def kernel(x, c, ctx, c_ctx, w_mod, b_mod, norm1_w, w_in, conv_w, conv_b, dt_bias, a_log, d_skip,
           ssd_norm_w, gmlp_norm_w, gmlp_ws, gmlp_bs, w_out, norm2_w, peer_wq, peer_keys, peer_u,
           peer_v, final_norm_w):
    b, seq, d = x.shape
    assert w_mod.shape[0] == 1, "single-layer block"
    assert seq % CHUNK == 0 and ctx.shape[1] % CHUNK == 0 and b + 1 <= SUBLANES
    tm, tp, tb, eb = _tile_sizes(seq)
    assert seq % tm == 0 and seq % tp == 0 and seq % tb == 0 and seq % SSD_STEP_ROWS == 0

    cc = jnp.zeros((SUBLANES, d), F32).at[:b].set(c).at[b].set(c_ctx)
    mod = _mod_call(cc, w_mod[0], b_mod[0][None, :]).reshape(SUBLANES, 6, d)
    mod_x = [mod[:b, k][:, None, :] for k in range(6)]
    shift_s, scale_s = mod[b, 0][None, :], mod[b, 1][None, :]

    nf = d // 4
    omega = (1.0 / (10000.0 ** (jnp.arange(nf, dtype=F32) / nf)))[None, :]
    pos_row, pos_col = _pos_call(seq, d, omega)

    w = w_in[0]
    o1, o2 = SSD_WIDTH, SSD_WIDTH + CONV_CH
    o3 = o2 + 2 * SSD_HEADS
    dt_pad = jnp.zeros((d, LANES - 2 * SSD_HEADS), F32)
    w_dt = jnp.concatenate([w[:, o2:o3], dt_pad], axis=1)
    w_main = jnp.concatenate([w[:, :o2], w[:, o3:], w_dt], axis=1).astype(BF16)
    w_xd = jnp.concatenate([w[:, o1:o2], w_dt], axis=1).astype(BF16)
    pad1 = jnp.zeros((1, LANES - 2 * SSD_HEADS), F32)
    dtb = jnp.concatenate([dt_bias[0].reshape(1, -1), pad1], axis=1)
    alog = jnp.concatenate([a_log[0].reshape(1, -1), pad1], axis=1)
    dskip = jnp.repeat(d_skip[0], SSD_HEAD_DIM)[None, :]
    cw, cb = conv_w[0], conv_b[0][None, :]
    n1 = norm1_w[0][None, :]

    st_f, st_b = _ctx_call(ctx, shift_s, scale_s, n1, w_xd, cw, cb, dtb, alog)

    gw = gmlp_ws[0].astype(BF16)
    gnw = gmlp_norm_w[0].reshape(1, GMLP_WIDTH)
    gb = jnp.repeat(gmlp_bs[0].T, GMLP_GROUP_DIM, axis=1)
    xp, z, xbc_raw, dt_raw, y_g = _inproj_call(x, pos_row, pos_col, mod_x[0], mod_x[1], n1, w_main, gw,
                                               gnw, gb, tm)

    y_f = _ssd_fwd_call(xbc_raw, dt_raw, cw, cb, dtb, alog, dskip, st_f)
    x1 = _ssd_bwd_call(xbc_raw, dt_raw, cw, cb, dtb, alog, st_b, y_f, z, y_g, xp, mod_x[2],
                       ssd_norm_w[0][None, :], w_out[0].astype(BF16))

    x1f = x1.reshape(b * seq, d)
    ht, c1, e1, code2, e2 = _peer_prep_call(
        x1f, mod_x[3], mod_x[4], norm2_w[0][None, :], peer_wq[0].astype(BF16).T,
        peer_keys[0].astype(BF16), tp, seq)
    vt_blocks = peer_v[0].astype(BF16).reshape(-1, eb, d).transpose(0, 2, 1)
    out = _peer_dense_call(ht, peer_u[0].astype(BF16), vt_blocks, c1, e1, code2, e2,
                           x1f, mod_x[5], final_norm_w[None, :], tb, eb, seq)
    return out.reshape(b, seq, d)
```

```python
import functools
import math

import jax
import jax.numpy as jnp
from jax import lax
from jax.experimental import pallas as pl
from jax.experimental.pallas import tpu as pltpu

F32 = jnp.float32
BF16 = jnp.bfloat16

GRID_W = 64
SSD_HEAD_DIM = 64
SSD_HEADS = 16
SSD_GROUPS = 2
HEADS_PER_GROUP = SSD_HEADS // SSD_GROUPS
SSD_STATE = 128
CHUNK = 128
SSD_WIDTH = SSD_HEADS * SSD_HEAD_DIM
GN = SSD_GROUPS * SSD_STATE
CONV_CH = SSD_WIDTH + 2 * GN
GMLP_GROUPS = 8
GMLP_GROUP_DIM = 128
GMLP_WIDTH = GMLP_GROUPS * GMLP_GROUP_DIM
PEER_KEYS = 128
PEER_HEADS = 8
PEER_HALF_DIM = 128
PEER_TOPK = 16
EPS = 1e-6

LANES = 128
SUBLANES = 8
BF16_ROWS = 16
VMEM_LIMIT = 56 * 1024 * 1024

GROUP_W = HEADS_PER_GROUP * SSD_HEAD_DIM
SSD_STEP_ROWS = 4 * CHUNK
GATE_LANE_GROUPS = 4
ACT_DOT_ROWS = 128


def _cparams(n_axes):
    return pltpu.CompilerParams(
        dimension_semantics=("arbitrary",) * n_axes,
        vmem_limit_bytes=VMEM_LIMIT)


def _rms(x, w):
    return x * lax.rsqrt(jnp.mean(x * x, axis=-1, keepdims=True) + EPS) * w


def _silu(x):
    return x * jax.nn.sigmoid(x)


def _dot(a, b):
    return jnp.dot(a, b, preferred_element_type=F32)


def _dot_nt(a, b):
    return lax.dot_general(a, b, (((1,), (1,)), ((), ())), preferred_element_type=F32)


def _mod_kernel(c_ref, w_ref, b_ref, o_ref):
    a = _silu(c_ref[...]).astype(BF16)
    o_ref[...] = _dot(a, w_ref[...].astype(BF16)) + b_ref[...]


def _mod_call(cc, w_mod, b_mod):
    d, n = w_mod.shape
    bn = 1536
    return pl.pallas_call(
        _mod_kernel,
        grid=(n // bn,),
        in_specs=[pl.BlockSpec((SUBLANES, d), lambda j: (0, 0)),
                  pl.BlockSpec((d, bn), lambda j: (0, j)),
                  pl.BlockSpec((1, bn), lambda j: (0, j))],
        out_specs=pl.BlockSpec((SUBLANES, bn), lambda j: (0, j)),
        out_shape=jax.ShapeDtypeStruct((SUBLANES, n), F32),
        compiler_params=_cparams(1),
        name="mod",
    )(cc, w_mod, b_mod)


def _pos_kernel(omega_ref, row_ref, col_ref):
    nf = omega_ref.shape[1]
    om = omega_ref[...]
    for ref in (row_ref, col_ref):
        ang = lax.broadcasted_iota(jnp.int32, (ref.shape[0], nf), 0).astype(F32) * om
        ref[:, :nf] = jnp.sin(ang)
        ref[:, nf:] = jnp.cos(ang)


def _pos_call(seq, dim, omega):
    half = dim // 2
    return pl.pallas_call(
        _pos_kernel,
        out_shape=(jax.ShapeDtypeStruct((seq // GRID_W, half), F32),
                   jax.ShapeDtypeStruct((GRID_W, half), F32)),
        name="pos",
    )(omega)


def _conv_silu(x, prev_row, next_row, w, b):
    n = x.shape[0]
    rows = lax.broadcasted_iota(jnp.int32, x.shape, 0)
    x_prev = jnp.where(rows == 0, prev_row, pltpu.roll(x, 1, axis=0))
    x_next = jnp.where(rows == n - 1, next_row, pltpu.roll(x, n - 1, axis=0))
    y = b + x_prev * w[0:1, :] + x * w[1:2, :] + x_next * w[2:3, :]
    return _silu(y)


def _softplus(x):
    return jnp.maximum(x, 0.0) + jnp.log1p(jnp.exp(-jnp.abs(x)))


def _split3(x):
    hi = x.astype(BF16)
    r = x - hi.astype(F32)
    mid = r.astype(BF16)
    return hi, mid, (r - mid.astype(F32)).astype(BF16)


def _dot_exact_rhs(x, m):
    return sum(_dot(p, m) for p in _split3(x))


def _dot_exact_lhs(m, x):
    return sum(_dot(m, p) for p in _split3(x))


def _ssd_chunk(xbc, dt_all, a_all, expand, h_ref, reverse, want_y):
    q = xbc.shape[0]
    col0 = SSD_HEADS if reverse else 0
    xs = xbc[:, :SSD_WIDTH]
    d_a = dt_all * a_all
    ti = lax.broadcasted_iota(jnp.int32, (q, q), 0)
    si = lax.broadcasted_iota(jnp.int32, (q, q), 1)
    keep = (si >= ti) if reverse else (si <= ti)
    tri = jnp.where(keep, 1.0, 0.0).astype(BF16)
    acs = _dot_exact_lhs(tri, d_a)
    last = 0 if reverse else q - 1
    total = acs[last:last + 1, :]
    stacked = jnp.concatenate([jnp.exp(acs), dt_all * jnp.exp(total - acs)], axis=0)
    ex = _dot_exact_rhs(stacked, expand)
    ea_x = ex[:q]
    xd = (xs * ex[q:]).astype(BF16)
    xs_b = xs.astype(BF16)
    chunk_decay = ea_x[last:last + 1, :]
    acs_t = acs.T if want_y else None
    dt_t = dt_all.T if want_y else None
    lane = lax.broadcasted_iota(jnp.int32, (q, LANES), 1)
    ys = []
    for g in range(SSD_GROUPS):
        bm = xbc[:, SSD_WIDTH + g * SSD_STATE:SSD_WIDTH + (g + 1) * SSD_STATE]
        cm = xbc[:, SSD_WIDTH + GN + g * SSD_STATE:SSD_WIDTH + GN + (g + 1) * SSD_STATE]
        gsl = slice(g * GROUP_W, (g + 1) * GROUP_W)
        h_old = h_ref[g]
        if want_y:
            cm_b = cm.astype(BF16)
            cb = _dot_nt(cm_b, bm.astype(BF16))
            y_off = _dot(cm_b, h_old.astype(BF16)) * ea_x[:, gsl]
            pieces = []
            for pr in range(HEADS_PER_GROUP // 2):
                ms = []
                for r in (2 * pr, 2 * pr + 1):
                    j = col0 + g * HEADS_PER_GROUP + r
                    seg = acs[:, j:j + 1] - acs_t[j:j + 1, :]
                    lm = jnp.exp(jnp.where(keep, seg, -jnp.inf))
                    ms.append((cb * lm * dt_t[j:j + 1, :]).astype(BF16))
                c0 = g * GROUP_W + pr * LANES
                xp = xs_b[:, c0:c0 + LANES]
                zero = jnp.zeros_like(xp)
                rhs = jnp.concatenate([jnp.where(lane < SSD_HEAD_DIM, xp, zero),
                                       jnp.where(lane >= SSD_HEAD_DIM, xp, zero)], axis=0)
                pieces.append(_dot(jnp.concatenate(ms, axis=1), rhs))
            ys.append(jnp.concatenate(pieces, axis=1) + y_off)
        h_ref[g] = h_old * chunk_decay[:, gsl] + _dot(bm.T.astype(BF16), xd[:, gsl])
    return jnp.concatenate(ys, axis=1) if want_y else None


def _expand_matrix(reverse):
    rows = lax.broadcasted_iota(jnp.int32, (LANES, SSD_WIDTH), 0)
    cols = lax.broadcasted_iota(jnp.int32, (LANES, SSD_WIDTH), 1)
    head = cols // SSD_HEAD_DIM + (SSD_HEADS if reverse else 0)
    return jnp.where(rows == head, 1.0, 0.0).astype(BF16)


def _ctx_kernel(ctx_ref, shift_ref, scale_ref, nw_ref, w_ref, cw_ref, cb_ref, dtb_ref, alog_ref,
                stf_ref, stb_ref):
    h = _rms(ctx_ref[0], nw_ref[...]) * (1.0 + scale_ref[...]) + shift_ref[...]
    proj = _dot(h.astype(BF16), w_ref[...])
    n = proj.shape[0]
    zero_row = jnp.zeros((1, CONV_CH), F32)
    xbc = _conv_silu(proj[:, :CONV_CH], zero_row, zero_row, cw_ref[...], cb_ref[...])
    dt_all = _softplus(proj[:, CONV_CH:] + dtb_ref[...])
    a_all = -jnp.exp(alog_ref[...])
    stf_ref[...] = jnp.zeros_like(stf_ref)
    stb_ref[...] = jnp.zeros_like(stb_ref)
    nck = n // CHUNK
    ef, eb = _expand_matrix(False), _expand_matrix(True)
    for ci in range(nck):
        sl = slice(ci * CHUNK, (ci + 1) * CHUNK)
        _ssd_chunk(xbc[sl], dt_all[sl], a_all, ef, stf_ref.at[0], False, False)
    for ci in reversed(range(nck)):
        sl = slice(ci * CHUNK, (ci + 1) * CHUNK)
        _ssd_chunk(xbc[sl], dt_all[sl], a_all, eb, stb_ref.at[0], True, False)


def _ctx_call(ctx, shift, scale, norm_w, w_xd, conv_w, conv_b, dtb, alog):
    b, n, d = ctx.shape
    st_shape = jax.ShapeDtypeStruct((b, SSD_GROUPS, SSD_STATE, GROUP_W), F32)
    st_spec = pl.BlockSpec((1, SSD_GROUPS, SSD_STATE, GROUP_W), lambda i: (i, 0, 0, 0))
    full = lambda a: pl.BlockSpec(a.shape, lambda i: (0,) * a.ndim)
    return pl.pallas_call(
        _ctx_kernel,
        grid=(b,),
        in_specs=[pl.BlockSpec((1, n, d), lambda i: (i, 0, 0)),
                  full(shift), full(scale), full(norm_w), full(w_xd), full(conv_w), full(conv_b),
                  full(dtb), full(alog)],
        out_specs=(st_spec, st_spec),
        out_shape=(st_shape, st_shape),
        compiler_params=_cparams(1),
        name="ctx_state",
    )(ctx, shift, scale, norm_w, w_xd, conv_w, conv_b, dtb, alog)


def _inproj_kernel(x_ref, prow_ref, pcol_ref, shift_ref, scale_ref, nw_ref, w_ref, gw_ref, gnw_ref,
                   gb_ref, xp_ref, z_ref, xbc_ref, dt_ref, yg_ref):
    pcol = pcol_ref[...]
    pos = jnp.concatenate(
        [jnp.concatenate([jnp.broadcast_to(prow_ref[k:k + 1, :], pcol.shape), pcol], axis=1)
         for k in range(prow_ref.shape[0])], axis=0)
    xp = x_ref[0] + pos
    xp_ref[0] = xp
    h = _rms(xp, nw_ref[...]) * (1.0 + scale_ref[0]) + shift_ref[0]
    proj = _dot(h.astype(BF16), w_ref[...])
    z_ref[0] = proj[:, :SSD_WIDTH]
    xbc_ref[0] = proj[:, SSD_WIDTH:SSD_WIDTH + CONV_CH]
    o = SSD_WIDTH + CONV_CH
    u = jax.nn.gelu(proj[:, o:o + GMLP_WIDTH])
    v = jax.nn.gelu(proj[:, o + GMLP_WIDTH:o + 2 * GMLP_WIDTH])
    dt_ref[0] = proj[:, o + 2 * GMLP_WIDTH:]
    tm = u.shape[0]
    gnw = gnw_ref[...]
    gb = gb_ref[...]
    for g in range(GMLP_GROUPS):
        gs = slice(g * GMLP_GROUP_DIM, (g + 1) * GMLP_GROUP_DIM)
        vn = _rms(v[:, gs], gnw[:, gs]).astype(BF16)
        wg = gw_ref[g]
        for ci in range(tm // CHUNK):
            rs = slice(ci * CHUNK, (ci + 1) * CHUNK)
            mixed = _dot(wg, vn[rs]) + gb[:, gs]
            yg_ref[0, rs, gs] = (u[rs, gs] * mixed).astype(BF16)


def _inproj_call(x, pos_row, pos_col, shift, scale, norm_w, w_main, gw, gnw, gb, tm):
    b, seq, d = x.shape
    nt = seq // tm
    assert tm % (SUBLANES * GRID_W) == 0
    full = lambda a: pl.BlockSpec(a.shape, lambda i, j: (0,) * a.ndim)
    tok = lambda w: pl.BlockSpec((1, tm, w), lambda i, j: (i, j, 0))
    per_b = pl.BlockSpec((1, 1, d), lambda i, j: (i, 0, 0))
    shp = lambda w, dt: jax.ShapeDtypeStruct((b, seq, w), dt)
    return pl.pallas_call(
        _inproj_kernel,
        grid=(b, nt),
        in_specs=[tok(d), pl.BlockSpec((tm // GRID_W, d // 2), lambda i, j: (j, 0)), full(pos_col),
                  per_b, per_b, full(norm_w), full(w_main), full(gw), full(gnw), full(gb)],
        out_specs=(tok(d), tok(SSD_WIDTH), tok(CONV_CH), tok(LANES), tok(GMLP_WIDTH)),
        out_shape=(shp(d, F32), shp(SSD_WIDTH, F32), shp(CONV_CH, F32), shp(LANES, F32),
                   shp(GMLP_WIDTH, BF16)),
        compiler_params=_cparams(2),
        name="inproj",
    )(x, pos_row, pos_col, shift, scale, norm_w, w_main, gw, gnw, gb)


def _halo_rows(prev_ref, next_ref, c, nc):
    prev_row = jnp.where(c > 0, prev_ref[0, SUBLANES - 1:SUBLANES, :], 0.0)
    next_row = jnp.where(c < nc - 1, next_ref[0, 0:1, :], 0.0)
    return prev_row, next_row


def _ssd_fwd_kernel(xbc_ref, prev_ref, next_ref, dt_ref, cw_ref, cb_ref, dtb_ref, alog_ref,
                    dskip_ref, h0_ref, y_ref, h_ref):
    c = pl.program_id(1)
    nc = pl.num_programs(1)

    @pl.when(c == 0)
    def _():
        h_ref[...] = h0_ref[0]

    prev_row, next_row = _halo_rows(prev_ref, next_ref, c, nc)
    xbc = _conv_silu(xbc_ref[0], prev_row, next_row, cw_ref[...], cb_ref[...])
    dt_all = _softplus(dt_ref[0] + dtb_ref[...])
    a_all = -jnp.exp(alog_ref[...])
    expand = _expand_matrix(False)
    for ci in range(xbc.shape[0] // CHUNK):
        rs = slice(ci * CHUNK, (ci + 1) * CHUNK)
        y = _ssd_chunk(xbc[rs], dt_all[rs], a_all, expand, h_ref, False, True)
        y_ref[0, rs, :] = y + dskip_ref[...] * xbc[rs, :SSD_WIDTH]


def _ssd_bwd_kernel(xbc_ref, prev_ref, next_ref, dt_ref, cw_ref, cb_ref, dtb_ref, alog_ref,
                    h0_ref, yf_ref, z_ref, yg_ref, xp_ref, gate_ref, snw_ref, wo_ref,
                    o_ref, h_ref, ys_ref):
    cr = pl.program_id(1)
    nc = pl.num_programs(1)
    c = nc - 1 - cr

    @pl.when(cr == 0)
    def _():
        h_ref[...] = h0_ref[0]

    prev_row, next_row = _halo_rows(prev_ref, next_ref, c, nc)
    xbc = _conv_silu(xbc_ref[0], prev_row, next_row, cw_ref[...], cb_ref[...])
    dt_all = _softplus(dt_ref[0] + dtb_ref[...])
    a_all = -jnp.exp(alog_ref[...])
    expand = _expand_matrix(True)
    for ci in reversed(range(xbc.shape[0] // CHUNK)):
        rs = slice(ci * CHUNK, (ci + 1) * CHUNK)
        y = yf_ref[0, rs, :] + _ssd_chunk(xbc[rs], dt_all[rs], a_all, expand, h_ref, True, True)
        ys_ref[rs, :] = _rms(y * _silu(z_ref[0, rs, :]), snw_ref[...]).astype(BF16)
    out = _dot(ys_ref[...], wo_ref[:SSD_WIDTH, :]) + _dot(yg_ref[0], wo_ref[SSD_WIDTH:, :])
    o_ref[0] = xp_ref[0] + gate_ref[0] * out


def _ssd_specs(seq, reverse):
    rows = min(SSD_STEP_ROWS, seq)
    nc = seq // rows
    hb = rows // SUBLANES
    last_hb = seq // SUBLANES - 1
    cidx = (lambda j: nc - 1 - j) if reverse else (lambda j: j)
    cur = lambda w: pl.BlockSpec((1, rows, w), lambda i, j: (i, cidx(j), 0))
    prev = pl.BlockSpec((1, SUBLANES, CONV_CH),
                        lambda i, j: (i, jnp.maximum(cidx(j) * hb - 1, 0), 0))
    nxt = pl.BlockSpec((1, SUBLANES, CONV_CH),
                       lambda i, j: (i, jnp.minimum((cidx(j) + 1) * hb, last_hb), 0))
    return nc, cur, prev, nxt


def _ssd_fwd_call(xbc_raw, dt_raw, conv_w, conv_b, dtb, alog, dskip, st_f):
    b, seq, _ = xbc_raw.shape
    nc, cur, prev, nxt = _ssd_specs(seq, False)
    full = lambda a: pl.BlockSpec(a.shape, lambda i, j: (0,) * a.ndim)
    st = pl.BlockSpec((1, SSD_GROUPS, SSD_STATE, GROUP_W), lambda i, j: (i, 0, 0, 0))
    return pl.pallas_call(
        _ssd_fwd_kernel,
        grid=(b, nc),
        in_specs=[cur(CONV_CH), prev, nxt, cur(LANES), full(conv_w), full(conv_b), full(dtb),
                  full(alog), full(dskip), st],
        out_specs=cur(SSD_WIDTH),
        out_shape=jax.ShapeDtypeStruct((b, seq, SSD_WIDTH), F32),
        scratch_shapes=[pltpu.VMEM((SSD_GROUPS, SSD_STATE, GROUP_W), F32)],
        compiler_params=_cparams(2),
        name="ssd_fwd",
    )(xbc_raw, xbc_raw, xbc_raw, dt_raw, conv_w, conv_b, dtb, alog, dskip, st_f)


def _ssd_bwd_call(xbc_raw, dt_raw, conv_w, conv_b, dtb, alog, st_b, y_f, z, y_g, xp, gate, snw, w_out):
    b, seq, d = xp.shape
    nc, cur, prev, nxt = _ssd_specs(seq, True)
    full = lambda a: pl.BlockSpec(a.shape, lambda i, j: (0,) * a.ndim)
    st = pl.BlockSpec((1, SSD_GROUPS, SSD_STATE, GROUP_W), lambda i, j: (i, 0, 0, 0))
    per_b = pl.BlockSpec((1, 1, d), lambda i, j: (i, 0, 0))
    return pl.pallas_call(
        _ssd_bwd_kernel,
        grid=(b, nc),
        in_specs=[cur(CONV_CH), prev, nxt, cur(LANES), full(conv_w), full(conv_b), full(dtb),
                  full(alog), st, cur(SSD_WIDTH), cur(SSD_WIDTH), cur(GMLP_WIDTH), cur(d), per_b,
                  full(snw), full(w_out)],
        out_specs=cur(d),
        out_shape=jax.ShapeDtypeStruct((b, seq, d), F32),
        scratch_shapes=[pltpu.VMEM((SSD_GROUPS, SSD_STATE, GROUP_W), F32),
                        pltpu.VMEM((min(SSD_STEP_ROWS, seq), SSD_WIDTH), BF16)],
        compiler_params=_cparams(2),
        name="ssd_bwd",
    )(xbc_raw, xbc_raw, xbc_raw, dt_raw, conv_w, conv_b, dtb, alog, st_b, y_f, z, y_g, xp, gate,
      snw, w_out)


def _oem_sort_pairs(n):
    pairs = []
    p = 1
    while p < n:
        k = p
        while k >= 1:
            for j in range(k % p, n - k, 2 * k):
                for i in range(min(k, n - j - k)):
                    if (i + j) // (2 * p) == (i + j + k) // (2 * p):
                        pairs.append((i + j, i + j + k))
            k //= 2
        p *= 2
    return pairs


def _cmp_exchange(x, y):
    if x is None:
        return y, None
    if y is None:
        return x, None
    return jnp.maximum(x, y), jnp.minimum(x, y)


def _merge_top(xs, ys, n):
    xs = list(xs) + [None] * (n - len(xs))
    ys = list(ys) + [None] * (n - len(ys))
    out = [_cmp_exchange(xs[k], ys[n - 1 - k])[0] for k in range(n)]
    stride = n // 2
    while stride >= 1:
        for i in range(n):
            if i & stride == 0:
                out[i], out[i + stride] = _cmp_exchange(out[i], out[i + stride])
        stride //= 2
    return [v for v in out if v is not None]


def _top16_sorted(vals):
    groups = []
    for g in range(0, len(vals), PEER_TOPK):
        grp = list(vals[g:g + PEER_TOPK])
        for i, j in _oem_sort_pairs(PEER_TOPK):
            grp[i], grp[j] = _cmp_exchange(grp[i], grp[j])
        groups.append(grp)
    while len(groups) > 1:
        groups = [_merge_top(groups[i], groups[i + 1], PEER_TOPK) for i in range(0, len(groups), 2)]
    return groups[0]


def _peer_prep_kernel(x_ref, shift_ref, scale_ref, nw_ref, wqt_ref, keys_ref,
                      ht_ref, c1_ref, e1_ref, code2_ref, e2_ref, q_scr, s_scr, o_scr):
    h = _rms(x_ref[...], nw_ref[...]) * (1.0 + scale_ref[0]) + shift_ref[0]
    ht = h.T.astype(BF16)
    ht_ref[...] = ht
    tp = ht.shape[1]
    ntg = tp // LANES
    assert ntg == SUBLANES
    q_rows_per_dot = 4 * PEER_HALF_DIM
    for r0 in range(0, q_scr.shape[0], q_rows_per_dot):
        q_scr[r0:r0 + q_rows_per_dot, :] = _dot(wqt_ref[r0:r0 + q_rows_per_dot, :], ht).astype(BF16)

    def head(hd, carry):
        vals = []
        for k in range(2):
            q_rows = pl.ds(pl.multiple_of((hd * 2 + k) * PEER_HALF_DIM, PEER_HALF_DIM), PEER_HALF_DIM)
            st = _dot(keys_ref[k], q_scr[q_rows, :])
            for tg in range(ntg):
                s_scr[k, tg * PEER_KEYS:(tg + 1) * PEER_KEYS, :] = st[:, tg * LANES:(tg + 1) * LANES]
            vals.append([s_scr[k, pl.ds(key, ntg, stride=PEER_KEYS), :] for key in range(PEER_KEYS)])
        a = _top16_sorted(vals[0])
        b = _top16_sorted(vals[1])
        rows = [[a[i] + b[j] for j in range(PEER_TOPK // (i + 1))] for i in range(PEER_TOPK // 2)]
        col0 = [a[i] + b[0] for i in range(PEER_TOPK // 2, PEER_TOPK)]
        top = _merge_top(rows[0], _merge_top(rows[1], col0, PEER_TOPK), PEER_TOPK)
        rest = _merge_top(_merge_top(rows[2], rows[3], PEER_TOPK),
                          _merge_top(_merge_top(rows[4], rows[5], PEER_TOPK),
                                     _merge_top(rows[6], rows[7], PEER_TOPK), PEER_TOPK), PEER_TOPK)
        top = _merge_top(top, rest, PEER_TOPK)
        thr = top[-1]
        zsum = sum(jnp.exp(t - top[0]) for t in top[1:]) + 1.0
        half_inv_z = 0.5 / zsum
        none = float(PEER_TOPK + 1)
        grid_rows = rows + [[v] for v in col0]
        above = [sum(jnp.where(v > thr, 1.0, 0.0) for v in row) for row in grid_rows]
        tied = [sum(jnp.where(v == thr, 1.0, 0.0) for v in row) for row in grid_rows]
        left = float(PEER_TOPK) - sum(above)
        c_rank = []
        for i in range(PEER_TOPK):
            take = jnp.minimum(tied[i], left)
            left = left - take
            c_rank.append(none - above[i] - take)
        for kb in range(0, PEER_KEYS, SUBLANES):
            keys = range(kb, kb + SUBLANES)
            c1 = {key: jnp.full_like(thr, none) for key in keys}
            code2 = {key: jnp.zeros_like(thr) for key in keys}
            for r in reversed(range(PEER_TOPK)):
                for key in keys:
                    c1[key] = jnp.where(vals[0][key] >= a[r], c_rank[r], c1[key])
                    code2[key] = jnp.where(vals[1][key] >= b[r], float(PEER_TOPK - r), code2[key])
            for key in keys:
                sl = slice(key * ntg, (key + 1) * ntg)
                o_scr[0, sl, :] = c1[key]
                o_scr[1, sl, :] = jnp.exp(vals[0][key] - a[0]) * half_inv_z
                o_scr[2, sl, :] = code2[key]
                o_scr[3, sl, :] = jnp.exp(vals[1][key] - b[0])
        for tg in range(ntg):
            rs = pl.ds(tg, PEER_KEYS, stride=ntg)
            ls = slice(tg * LANES, (tg + 1) * LANES)
            c1_ref[hd, tg] = o_scr[0, rs, :]
            e1_ref[hd, tg] = o_scr[1, rs, :]
            code2_ref[hd, :, ls] = o_scr[2, rs, :].astype(BF16)
            e2_ref[hd, :, ls] = o_scr[3, rs, :].astype(BF16)
        return carry

    lax.fori_loop(0, PEER_HEADS, head, 0)


def _peer_prep_call(x1, shift, scale, norm_w, w_qt, keys, tp, seq):
    t, d = x1.shape
    per_seq = seq // tp
    full = lambda a: pl.BlockSpec(a.shape, lambda i: (0,) * a.ndim)
    per_b = pl.BlockSpec((1, 1, d), lambda i: (i // per_seq, 0, 0))
    hk = pl.BlockSpec((PEER_HEADS, PEER_KEYS, tp), lambda i: (0, 0, i))
    hk_shape = lambda dt: jax.ShapeDtypeStruct((PEER_HEADS, PEER_KEYS, t), dt)
    hr = pl.BlockSpec((PEER_HEADS, tp // LANES, PEER_KEYS, LANES), lambda i: (0, i, 0, 0))
    hr_shape = jax.ShapeDtypeStruct((PEER_HEADS, t // LANES, PEER_KEYS, LANES), F32)
    rows = (tp // LANES) * PEER_KEYS
    return pl.pallas_call(
        _peer_prep_kernel,
        grid=(t // tp,),
        in_specs=[pl.BlockSpec((tp, d), lambda i: (i, 0)), per_b, per_b, full(norm_w),
                  pl.BlockSpec(w_qt.shape, lambda i: (0, 0), pipeline_mode=pl.Buffered(1)),
                  full(keys)],
        out_specs=(pl.BlockSpec((d, tp), lambda i: (0, i)), hr, hr, hk, hk),
        out_shape=(jax.ShapeDtypeStruct((d, t), BF16), hr_shape, hr_shape,
                   hk_shape(BF16), hk_shape(BF16)),
        scratch_shapes=[pltpu.VMEM((w_qt.shape[0], tp), BF16), pltpu.VMEM((2, rows, LANES), F32),
                        pltpu.VMEM((4, rows, LANES), F32)],
        compiler_params=_cparams(1),
        name="peer_prep",
    )(x1, shift, scale, norm_w, w_qt, keys)


def _gelu_x2(x):
    k = math.sqrt(2.0 / math.pi)
    inner = x * (x * x * (0.044715 * k) + k)
    return x * jnp.tanh(inner) + x


def _peer_dense_kernel(n_eb, ht_ref, u_ref, vt_ref, c1_ref, e1_ref, code2_ref, e2_ref, x_ref,
                       gate_ref, fnw_ref, o_ref, acc_ref, act_a_ref, act_b_ref, wg_ref):
    s = pl.program_id(0)
    jv = lax.rem(jnp.maximum(s - 1, 0), n_eb)
    slot = lax.rem(s, 2)

    @pl.when(s == 0)
    def _():
        act_b_ref[...] = jnp.zeros_like(act_b_ref)

    @pl.when(jv == 0)
    def _():
        acc_ref[...] = jnp.zeros_like(acc_ref)

    tb = act_a_ref.shape[1]
    n_tiles = PEER_KEYS // BF16_ROWS

    def step(act_new, act_prev):
        per_dot = ACT_DOT_ROWS // PEER_KEYS
        for i1 in range(c1_ref.shape[2]):
            rows = slice(i1 * PEER_KEYS, (i1 + 1) * PEER_KEYS)
            if i1 % per_dot == 0:
                dot_rows = slice(i1 * PEER_KEYS, i1 * PEER_KEYS + ACT_DOT_ROWS)
                act_rows = _gelu_x2(_dot(u_ref[dot_rows, :], ht_ref[...])).astype(BF16)
            bcast = pl.ds(i1, BF16_ROWS, stride=0)
            for g0 in range(0, tb // LANES, GATE_LANE_GROUPS):
                lts = range(g0, g0 + GATE_LANE_GROUPS)
                ls = slice(g0 * LANES, (g0 + GATE_LANE_GROUPS) * LANES)
                w = [None] * n_tiles
                for hd in range(PEER_HEADS):
                    c_row = jnp.concatenate([c1_ref[hd, lt, bcast, :] for lt in lts],
                                            axis=1).astype(BF16)
                    e_row = jnp.concatenate([e1_ref[hd, lt, bcast, :] for lt in lts],
                                            axis=1).astype(BF16)
                    for it in range(n_tiles):
                        rs = slice(it * BF16_ROWS, (it + 1) * BF16_ROWS)
                        sel = code2_ref[hd, rs, ls] >= c_row
                        term = jnp.where(sel, e2_ref[hd, rs, ls] * e_row, jnp.zeros_like(e_row))
                        w[it] = term if w[it] is None else w[it] + term
                wg_ref[rows, ls] = jnp.concatenate(w, axis=0) * act_prev[rows, ls]
            if i1 % per_dot == per_dot - 1:
                act_new[dot_rows, :] = act_rows
        acc_ref[...] += _dot(vt_ref[0], wg_ref[...])

    pl.when(slot == 0)(lambda: step(act_a_ref, act_b_ref))
    pl.when(slot == 1)(lambda: step(act_b_ref, act_a_ref))

    @pl.when(jnp.logical_and(s > 0, jv == n_eb - 1))
    def _():
        x2 = x_ref[...] + gate_ref[0] * acc_ref[...].T
        o_ref[...] = _rms(x2, fnw_ref[...])


def _peer_dense_call(ht, u_b, vt_b, c1, e1, code2, e2, x1, gate, fnw, tb, eb, seq):
    d, t = ht.shape
    e = u_b.shape[0]
    per_seq = seq // tb
    n_i1 = eb // PEER_KEYS
    n_eb = e // eb
    total = (t // tb) * n_eb
    ti1 = lambda s: jnp.minimum(s, total - 1) // n_eb
    ej1 = lambda s: jnp.minimum(s, total - 1) % n_eb
    ti2 = lambda s: jnp.maximum(s - 1, 0) // n_eb
    ej2 = lambda s: jnp.maximum(s - 1, 0) % n_eb
    rows1 = pl.BlockSpec((PEER_HEADS, tb // LANES, n_i1, LANES), lambda s: (0, ti2(s), ej2(s), 0))
    rows2 = pl.BlockSpec((PEER_HEADS, PEER_KEYS, tb), lambda s: (0, 0, ti2(s)))
    return pl.pallas_call(
        functools.partial(_peer_dense_kernel, n_eb),
        grid=(total + 1,),
        in_specs=[pl.BlockSpec((d, tb), lambda s: (0, ti1(s))),
                  pl.BlockSpec((eb, d), lambda s: (ej1(s), 0)),
                  pl.BlockSpec((1, d, eb), lambda s: (ej2(s), 0, 0)),
                  rows1, rows1, rows2, rows2,
                  pl.BlockSpec((tb, d), lambda s: (ti2(s), 0), pipeline_mode=pl.Buffered(1)),
                  pl.BlockSpec((1, 1, d), lambda s: (ti2(s) // per_seq, 0, 0)),
                  pl.BlockSpec((1, d), lambda s: (0, 0))],
        out_specs=pl.BlockSpec((tb, d), lambda s: (ti2(s), 0)),
        out_shape=jax.ShapeDtypeStruct((t, d), F32),
        scratch_shapes=[pltpu.VMEM((d, tb), F32), pltpu.VMEM((eb, tb), BF16),
                        pltpu.VMEM((eb, tb), BF16), pltpu.VMEM((eb, tb), BF16)],
        compiler_params=_cparams(1),
        name="peer_dense",
    )(ht, u_b, vt_b, c1, e1, code2, e2, x1, gate, fnw)


def _tile_sizes(seq):
    tm = min(512, seq)
    return tm, SUBLANES * LANES, min(512, seq), 2048


def kernel(x, c, ctx, c_ctx, w_mod, b_mod, norm1_w, w_in, conv_w, conv_b, dt_bias, a_log, d_skip,
           ssd_norm_w, gmlp_norm_w, gmlp_ws, gmlp_bs, w_out, norm2_w, peer_wq, peer_keys, peer_u,
           peer_v, final_norm_w):
    b, seq, d = x.shape
    assert w_mod.shape[0] == 1, "single-layer block"
    assert seq % CHUNK == 0 and ctx.shape[1] % CHUNK == 0 and b + 1 <= SUBLANES
    tm, tp, tb, eb = _tile_sizes(seq)
    assert seq % tm == 0 and seq % tp == 0 and seq % tb == 0 and seq % SSD_STEP_ROWS == 0

    cc = jnp.zeros((SUBLANES, d), F32).at[:b].set(c).at[b].set(c_ctx)
    mod = _mod_call(cc, w_mod[0], b_mod[0][None, :]).reshape(SUBLANES, 6, d)
    mod_x = [mod[:b, k][:, None, :] for k in range(6)]
    shift_s, scale_s = mod[b, 0][None, :], mod[b, 1][None, :]

    nf = d // 4
    omega = (1.0 / (10000.0 ** (jnp.arange(nf, dtype=F32) / nf)))[None, :]
    pos_row, pos_col = _pos_call(seq, d, omega)

    w = w_in[0]
    o1, o2 = SSD_WIDTH, SSD_WIDTH + CONV_CH
    o3 = o2 + 2 * SSD_HEADS
    dt_pad = jnp.zeros((d, LANES - 2 * SSD_HEADS), F32)
    w_dt = jnp.concatenate([w[:, o2:o3], dt_pad], axis=1)
    w_main = jnp.concatenate([w[:, :o2], w[:, o3:], w_dt], axis=1).astype(BF16)
    w_xd = jnp.concatenate([w[:, o1:o2], w_dt], axis=1).astype(BF16)
    pad1 = jnp.zeros((1, LANES - 2 * SSD_HEADS), F32)
    dtb = jnp.concatenate([dt_bias[0].reshape(1, -1), pad1], axis=1)
    alog = jnp.concatenate([a_log[0].reshape(1, -1), pad1], axis=1)
    dskip = jnp.repeat(d_skip[0], SSD_HEAD_DIM)[None, :]
    cw, cb = conv_w[0], conv_b[0][None, :]
    n1 = norm1_w[0][None, :]

    st_f, st_b = _ctx_call(ctx, shift_s, scale_s, n1, w_xd, cw, cb, dtb, alog)

    gw = gmlp_ws[0].astype(BF16)
    gnw = gmlp_norm_w[0].reshape(1, GMLP_WIDTH)
    gb = jnp.repeat(gmlp_bs[0].T, GMLP_GROUP_DIM, axis=1)
    xp, z, xbc_raw, dt_raw, y_g = _inproj_call(x, pos_row, pos_col, mod_x[0], mod_x[1], n1, w_main, gw,
                                               gnw, gb, tm)

    y_f = _ssd_fwd_call(xbc_raw, dt_raw, cw, cb, dtb, alog, dskip, st_f)
    x1 = _ssd_bwd_call(xbc_raw, dt_raw, cw, cb, dtb, alog, st_b, y_f, z, y_g, xp, mod_x[2],
                       ssd_norm_w[0][None, :], w_out[0].astype(BF16))

    x1f = x1.reshape(b * seq, d)
    ht, c1, e1, code2, e2 = _peer_prep_call(
        x1f, mod_x[3], mod_x[4], norm2_w[0][None, :], peer_wq[0].astype(BF16).T,
        peer_keys[0].astype(BF16), tp, seq)
    vt_blocks = peer_v[0].astype(BF16).reshape(-1, eb, d).transpose(0, 2, 1)
    out = _peer_dense_call(ht, peer_u[0].astype(BF16), vt_blocks, c1, e1, code2, e2,
                           x1f, mod_x[5], final_norm_w[None, :], tb, eb, seq)
    return out.reshape(b, seq, d)
```

```python
import functools
import math

import jax
import jax.numpy as jnp
from jax import lax
from jax.experimental import pallas as pl
from jax.experimental.pallas import tpu as pltpu

F32 = jnp.float32
BF16 = jnp.bfloat16

GRID_W = 64
SSD_HEAD_DIM = 64
SSD_HEADS = 16
SSD_GROUPS = 2
HEADS_PER_GROUP = SSD_HEADS // SSD_GROUPS
SSD_STATE = 128
CHUNK = 128
SSD_WIDTH = SSD_HEADS * SSD_HEAD_DIM
GN = SSD_GROUPS * SSD_STATE
CONV_CH = SSD_WIDTH + 2 * GN
GMLP_GROUPS = 8
GMLP_GROUP_DIM = 128
GMLP_WIDTH = GMLP_GROUPS * GMLP_GROUP_DIM
PEER_KEYS = 128
PEER_HEADS = 8
PEER_HALF_DIM = 128
PEER_TOPK = 16
EPS = 1e-6

LANES = 128
SUBLANES = 8
BF16_ROWS = 16
VMEM_LIMIT = 56 * 1024 * 1024

GROUP_W = HEADS_PER_GROUP * SSD_HEAD_DIM
SSD_STEP_ROWS = 4 * CHUNK
GATE_LANE_GROUPS = 4
ACT_DOT_ROWS = 128


def _cparams(n_axes):
    return pltpu.CompilerParams(
        dimension_semantics=("arbitrary",) * n_axes,
        vmem_limit_bytes=VMEM_LIMIT)


def _rms(x, w):
    return x * lax.rsqrt(jnp.mean(x * x, axis=-1, keepdims=True) + EPS) * w


def _silu(x):
    return x * jax.nn.sigmoid(x)


def _dot(a, b):
    return jnp.dot(a, b, preferred_element_type=F32)


def _dot_nt(a, b):
    return lax.dot_general(a, b, (((1,), (1,)), ((), ())), preferred_element_type=F32)


def _mod_kernel(c_ref, w_ref, b_ref, o_ref):
    a = _silu(c_ref[...]).astype(BF16)
    o_ref[...] = _dot(a, w_ref[...].astype(BF16)) + b_ref[...]


def _mod_call(cc, w_mod, b_mod):
    d, n = w_mod.shape
    bn = 1536
    return pl.pallas_call(
        _mod_kernel,
        grid=(n // bn,),
        in_specs=[pl.BlockSpec((SUBLANES, d), lambda j: (0, 0)),
                  pl.BlockSpec((d, bn), lambda j: (0, j)),
                  pl.BlockSpec((1, bn), lambda j: (0, j))],
        out_specs=pl.BlockSpec((SUBLANES, bn), lambda j: (0, j)),
        out_shape=jax.ShapeDtypeStruct((SUBLANES, n), F32),
        compiler_params=_cparams(1),
        name="mod",
    )(cc, w_mod, b_mod)


def _pos_kernel(omega_ref, row_ref, col_ref):
    nf = omega_ref.shape[1]
    om = omega_ref[...]
    for ref in (row_ref, col_ref):
        ang = lax.broadcasted_iota(jnp.int32, (ref.shape[0], nf), 0).astype(F32) * om
        ref[:, :nf] = jnp.sin(ang)
        ref[:, nf:] = jnp.cos(ang)


def _pos_call(seq, dim, omega):
    half = dim // 2
    return pl.pallas_call(
        _pos_kernel,
        out_shape=(jax.ShapeDtypeStruct((seq // GRID_W, half), F32),
                   jax.ShapeDtypeStruct((GRID_W, half), F32)),
        name="pos",
    )(omega)


def _conv_silu(x, prev_row, next_row, w, b):
    n = x.shape[0]
    rows = lax.broadcasted_iota(jnp.int32, x.shape, 0)
    x_prev = jnp.where(rows == 0, prev_row, pltpu.roll(x, 1, axis=0))
    x_next = jnp.where(rows == n - 1, next_row, pltpu.roll(x, n - 1, axis=0))
    y = b + x_prev * w[0:1, :] + x * w[1:2, :] + x_next * w[2:3, :]
    return _silu(y)


def _softplus(x):
    return jnp.maximum(x, 0.0) + jnp.log1p(jnp.exp(-jnp.abs(x)))


def _split3(x):
    hi = x.astype(BF16)
    r = x - hi.astype(F32)
    mid = r.astype(BF16)
    return hi, mid, (r - mid.astype(F32)).astype(BF16)


def _dot_exact_rhs(x, m):
    return sum(_dot(p, m) for p in _split3(x))


def _dot_exact_lhs(m, x):
    return sum(_dot(m, p) for p in _split3(x))


def _ssd_chunk(xbc, dt_all, a_all, expand, h_ref, reverse, want_y):
    q = xbc.shape[0]
    col0 = SSD_HEADS if reverse else 0
    xs = xbc[:, :SSD_WIDTH]
    d_a = dt_all * a_all
    ti = lax.broadcasted_iota(jnp.int32, (q, q), 0)
    si = lax.broadcasted_iota(jnp.int32, (q, q), 1)
    keep = (si >= ti) if reverse else (si <= ti)
    tri = jnp.where(keep, 1.0, 0.0).astype(BF16)
    acs = _dot_exact_lhs(tri, d_a)
    last = 0 if reverse else q - 1
    total = acs[last:last + 1, :]
    stacked = jnp.concatenate([jnp.exp(acs), dt_all * jnp.exp(total - acs)], axis=0)
    ex = _dot_exact_rhs(stacked, expand)
    ea_x = ex[:q]
    xd = (xs * ex[q:]).astype(BF16)
    xs_b = xs.astype(BF16)
    chunk_decay = ea_x[last:last + 1, :]
    acs_t = acs.T if want_y else None
    dt_t = dt_all.T if want_y else None
    lane = lax.broadcasted_iota(jnp.int32, (q, LANES), 1)
    ys = []
    for g in range(SSD_GROUPS):
        bm = xbc[:, SSD_WIDTH + g * SSD_STATE:SSD_WIDTH + (g + 1) * SSD_STATE]
        cm = xbc[:, SSD_WIDTH + GN + g * SSD_STATE:SSD_WIDTH + GN + (g + 1) * SSD_STATE]
        gsl = slice(g * GROUP_W, (g + 1) * GROUP_W)
        h_old = h_ref[g]
        if want_y:
            cm_b = cm.astype(BF16)
            cb = _dot_nt(cm_b, bm.astype(BF16))
            y_off = _dot(cm_b, h_old.astype(BF16)) * ea_x[:, gsl]
            pieces = []
            for pr in range(HEADS_PER_GROUP // 2):
                ms = []
                for r in (2 * pr, 2 * pr + 1):
                    j = col0 + g * HEADS_PER_GROUP + r
                    seg = acs[:, j:j + 1] - acs_t[j:j + 1, :]
                    lm = jnp.exp(jnp.where(keep, seg, -jnp.inf))
                    ms.append((cb * lm * dt_t[j:j + 1, :]).astype(BF16))
                c0 = g * GROUP_W + pr * LANES
                xp = xs_b[:, c0:c0 + LANES]
                zero = jnp.zeros_like(xp)
                rhs = jnp.concatenate([jnp.where(lane < SSD_HEAD_DIM, xp, zero),
                                       jnp.where(lane >= SSD_HEAD_DIM, xp, zero)], axis=0)
                pieces.append(_dot(jnp.concatenate(ms, axis=1), rhs))
            ys.append(jnp.concatenate(pieces, axis=1) + y_off)
        h_ref[g] = h_old * chunk_decay[:, gsl] + _dot(bm.T.astype(BF16), xd[:, gsl])
    return jnp.concatenate(ys, axis=1) if want_y else None


def _expand_matrix(reverse):
    rows = lax.broadcasted_iota(jnp.int32, (LANES, SSD_WIDTH), 0)
    cols = lax.broadcasted_iota(jnp.int32, (LANES, SSD_WIDTH), 1)
    head = cols // SSD_HEAD_DIM + (SSD_HEADS if reverse else 0)
    return jnp.where(rows == head, 1.0, 0.0).astype(BF16)


def _ctx_kernel(ctx_ref, shift_ref, scale_ref, nw_ref, w_ref, cw_ref, cb_ref, dtb_ref, alog_ref,
                stf_ref, stb_ref):
    h = _rms(ctx_ref[0], nw_ref[...]) * (1.0 + scale_ref[...]) + shift_ref[...]
    proj = _dot(h.astype(BF16), w_ref[...])
    n = proj.shape[0]
    zero_row = jnp.zeros((1, CONV_CH), F32)
    xbc = _conv_silu(proj[:, :CONV_CH], zero_row, zero_row, cw_ref[...], cb_ref[...])
    dt_all = _softplus(proj[:, CONV_CH:] + dtb_ref[...])
    a_all = -jnp.exp(alog_ref[...])
    stf_ref[...] = jnp.zeros_like(stf_ref)
    stb_ref[...] = jnp.zeros_like(stb_ref)
    nck = n // CHUNK
    ef, eb = _expand_matrix(False), _expand_matrix(True)
    for ci in range(nck):
        sl = slice(ci * CHUNK, (ci + 1) * CHUNK)
        _ssd_chunk(xbc[sl], dt_all[sl], a_all, ef, stf_ref.at[0], False, False)
    for ci in reversed(range(nck)):
        sl = slice(ci * CHUNK, (ci + 1) * CHUNK)
        _ssd_chunk(xbc[sl], dt_all[sl], a_all, eb, stb_ref.at[0], True, False)


def _ctx_call(ctx, shift, scale, norm_w, w_xd, conv_w, conv_b, dtb, alog):
    b, n, d = ctx.shape
    st_shape = jax.ShapeDtypeStruct((b, SSD_GROUPS, SSD_STATE, GROUP_W), F32)
    st_spec = pl.BlockSpec((1, SSD_GROUPS, SSD_STATE, GROUP_W), lambda i: (i, 0, 0, 0))
    full = lambda a: pl.BlockSpec(a.shape, lambda i: (0,) * a.ndim)
    return pl.pallas_call(
        _ctx_kernel,
        grid=(b,),
        in_specs=[pl.BlockSpec((1, n, d), lambda i: (i, 0, 0)),
                  full(shift), full(scale), full(norm_w), full(w_xd), full(conv_w), full(conv_b),
                  full(dtb), full(alog)],
        out_specs=(st_spec, st_spec),
        out_shape=(st_shape, st_shape),
        compiler_params=_cparams(1),
        name="ctx_state",
    )(ctx, shift, scale, norm_w, w_xd, conv_w, conv_b, dtb, alog)


def _inproj_kernel(x_ref, prow_ref, pcol_ref, shift_ref, scale_ref, nw_ref, w_ref, gw_ref, gnw_ref,
                   gb_ref, xp_ref, z_ref, xbc_ref, dt_ref, yg_ref):
    pcol = pcol_ref[...]
    pos = jnp.concatenate(
        [jnp.concatenate([jnp.broadcast_to(prow_ref[k:k + 1, :], pcol.shape), pcol], axis=1)
         for k in range(prow_ref.shape[0])], axis=0)
    xp = x_ref[0] + pos
    xp_ref[0] = xp
    h = _rms(xp, nw_ref[...]) * (1.0 + scale_ref[0]) + shift_ref[0]
    proj = _dot(h.astype(BF16), w_ref[...])
    z_ref[0] = proj[:, :SSD_WIDTH]
    xbc_ref[0] = proj[:, SSD_WIDTH:SSD_WIDTH + CONV_CH]
    o = SSD_WIDTH + CONV_CH
    u = jax.nn.gelu(proj[:, o:o + GMLP_WIDTH])
    v = jax.nn.gelu(proj[:, o + GMLP_WIDTH:o + 2 * GMLP_WIDTH])
    dt_ref[0] = proj[:, o + 2 * GMLP_WIDTH:]
    tm = u.shape[0]
    gnw = gnw_ref[...]
    gb = gb_ref[...]
    for g in range(GMLP_GROUPS):
        gs = slice(g * GMLP_GROUP_DIM, (g + 1) * GMLP_GROUP_DIM)
        vn = _rms(v[:, gs], gnw[:, gs]).astype(BF16)
        wg = gw_ref[g]
        for ci in range(tm // CHUNK):
            rs = slice(ci * CHUNK, (ci + 1) * CHUNK)
            mixed = _dot(wg, vn[rs]) + gb[:, gs]
            yg_ref[0, rs, gs] = (u[rs, gs] * mixed).astype(BF16)


def _inproj_call(x, pos_row, pos_col, shift, scale, norm_w, w_main, gw, gnw, gb, tm):
    b, seq, d = x.shape
    nt = seq // tm
    assert tm % (SUBLANES * GRID_W) == 0
    full = lambda a: pl.BlockSpec(a.shape, lambda i, j: (0,) * a.ndim)
    tok = lambda w: pl.BlockSpec((1, tm, w), lambda i, j: (i, j, 0))
    per_b = pl.BlockSpec((1, 1, d), lambda i, j: (i, 0, 0))
    shp = lambda w, dt: jax.ShapeDtypeStruct((b, seq, w), dt)
    return pl.pallas_call(
        _inproj_kernel,
        grid=(b, nt),
        in_specs=[tok(d), pl.BlockSpec((tm // GRID_W, d // 2), lambda i, j: (j, 0)), full(pos_col),
                  per_b, per_b, full(norm_w), full(w_main), full(gw), full(gnw), full(gb)],
        out_specs=(tok(d), tok(SSD_WIDTH), tok(CONV_CH), tok(LANES), tok(GMLP_WIDTH)),
        out_shape=(shp(d, F32), shp(SSD_WIDTH, F32), shp(CONV_CH, F32), shp(LANES, F32),
                   shp(GMLP_WIDTH, BF16)),
        compiler_params=_cparams(2),
        name="inproj",
    )(x, pos_row, pos_col, shift, scale, norm_w, w_main, gw, gnw, gb)


def _halo_rows(prev_ref, next_ref, c, nc):
    prev_row = jnp.where(c > 0, prev_ref[0, SUBLANES - 1:SUBLANES, :], 0.0)
    next_row = jnp.where(c < nc - 1, next_ref[0, 0:1, :], 0.0)
    return prev_row, next_row


def _ssd_fwd_kernel(xbc_ref, prev_ref, next_ref, dt_ref, cw_ref, cb_ref, dtb_ref, alog_ref,
                    dskip_ref, h0_ref, y_ref, h_ref):
    c = pl.program_id(1)
    nc = pl.num_programs(1)

    @pl.when(c == 0)
    def _():
        h_ref[...] = h0_ref[0]

    prev_row, next_row = _halo_rows(prev_ref, next_ref, c, nc)
    xbc = _conv_silu(xbc_ref[0], prev_row, next_row, cw_ref[...], cb_ref[...])
    dt_all = _softplus(dt_ref[0] + dtb_ref[...])
    a_all = -jnp.exp(alog_ref[...])
    expand = _expand_matrix(False)
    for ci in range(xbc.shape[0] // CHUNK):
        rs = slice(ci * CHUNK, (ci + 1) * CHUNK)
        y = _ssd_chunk(xbc[rs], dt_all[rs], a_all, expand, h_ref, False, True)
        y_ref[0, rs, :] = y + dskip_ref[...] * xbc[rs, :SSD_WIDTH]


def _ssd_bwd_kernel(xbc_ref, prev_ref, next_ref, dt_ref, cw_ref, cb_ref, dtb_ref, alog_ref,
                    h0_ref, yf_ref, z_ref, yg_ref, xp_ref, gate_ref, snw_ref, wo_ref,
                    o_ref, h_ref, ys_ref):
    cr = pl.program_id(1)
    nc = pl.num_programs(1)
    c = nc - 1 - cr

    @pl.when(cr == 0)
    def _():
        h_ref[...] = h0_ref[0]

    prev_row, next_row = _halo_rows(prev_ref, next_ref, c, nc)
    xbc = _conv_silu(xbc_ref[0], prev_row, next_row, cw_ref[...], cb_ref[...])
    dt_all = _softplus(dt_ref[0] + dtb_ref[...])
    a_all = -jnp.exp(alog_ref[...])
    expand = _expand_matrix(True)
    for ci in reversed(range(xbc.shape[0] // CHUNK)):
        rs = slice(ci * CHUNK, (ci + 1) * CHUNK)
        y = yf_ref[0, rs, :] + _ssd_chunk(xbc[rs], dt_all[rs], a_all, expand, h_ref, True, True)
        ys_ref[rs, :] = _rms(y * _silu(z_ref[0, rs, :]), snw_ref[...]).astype(BF16)
    out = _dot(ys_ref[...], wo_ref[:SSD_WIDTH, :]) + _dot(yg_ref[0], wo_ref[SSD_WIDTH:, :])
    o_ref[0] = xp_ref[0] + gate_ref[0] * out


def _ssd_specs(seq, reverse):
    rows = min(SSD_STEP_ROWS, seq)
    nc = seq // rows
    hb = rows // SUBLANES
    last_hb = seq // SUBLANES - 1
    cidx = (lambda j: nc - 1 - j) if reverse else (lambda j: j)
    cur = lambda w: pl.BlockSpec((1, rows, w), lambda i, j: (i, cidx(j), 0))
    prev = pl.BlockSpec((1, SUBLANES, CONV_CH),
                        lambda i, j: (i, jnp.maximum(cidx(j) * hb - 1, 0), 0))
    nxt = pl.BlockSpec((1, SUBLANES, CONV_CH),
                       lambda i, j: (i, jnp.minimum((cidx(j) + 1) * hb, last_hb), 0))
    return nc, cur, prev, nxt


def _ssd_fwd_call(xbc_raw, dt_raw, conv_w, conv_b, dtb, alog, dskip, st_f):
    b, seq, _ = xbc_raw.shape
    nc, cur, prev, nxt = _ssd_specs(seq, False)
    full = lambda a: pl.BlockSpec(a.shape, lambda i, j: (0,) * a.ndim)
    st = pl.BlockSpec((1, SSD_GROUPS, SSD_STATE, GROUP_W), lambda i, j: (i, 0, 0, 0))
    return pl.pallas_call(
        _ssd_fwd_kernel,
        grid=(b, nc),
        in_specs=[cur(CONV_CH), prev, nxt, cur(LANES), full(conv_w), full(conv_b), full(dtb),
                  full(alog), full(dskip), st],
        out_specs=cur(SSD_WIDTH),
        out_shape=jax.ShapeDtypeStruct((b, seq, SSD_WIDTH), F32),
        scratch_shapes=[pltpu.VMEM((SSD_GROUPS, SSD_STATE, GROUP_W), F32)],
        compiler_params=_cparams(2),
        name="ssd_fwd",
    )(xbc_raw, xbc_raw, xbc_raw, dt_raw, conv_w, conv_b, dtb, alog, dskip, st_f)


def _ssd_bwd_call(xbc_raw, dt_raw, conv_w, conv_b, dtb, alog, st_b, y_f, z, y_g, xp, gate, snw, w_out):
    b, seq, d = xp.shape
    nc, cur, prev, nxt = _ssd_specs(seq, True)
    full = lambda a: pl.BlockSpec(a.shape, lambda i, j: (0,) * a.ndim)
    st = pl.BlockSpec((1, SSD_GROUPS, SSD_STATE, GROUP_W), lambda i, j: (i, 0, 0, 0))
    per_b = pl.BlockSpec((1, 1, d), lambda i, j: (i, 0, 0))
    return pl.pallas_call(
        _ssd_bwd_kernel,
        grid=(b, nc),
        in_specs=[cur(CONV_CH), prev, nxt, cur(LANES), full(conv_w), full(conv_b), full(dtb),
                  full(alog), st, cur(SSD_WIDTH), cur(SSD_WIDTH), cur(GMLP_WIDTH), cur(d), per_b,
                  full(snw), full(w_out)],
        out_specs=cur(d),
        out_shape=jax.ShapeDtypeStruct((b, seq, d), F32),
        scratch_shapes=[pltpu.VMEM((SSD_GROUPS, SSD_STATE, GROUP_W), F32),
                        pltpu.VMEM((min(SSD_STEP_ROWS, seq), SSD_WIDTH), BF16)],
        compiler_params=_cparams(2),
        name="ssd_bwd",
    )(xbc_raw, xbc_raw, xbc_raw, dt_raw, conv_w, conv_b, dtb, alog, st_b, y_f, z, y_g, xp, gate,
      snw, w_out)


def _oem_sort_pairs(n):
    pairs = []
    p = 1
    while p < n:
        k = p
        while k >= 1:
            for j in range(k % p, n - k, 2 * k):
                for i in range(min(k, n - j - k)):
                    if (i + j) // (2 * p) == (i + j + k) // (2 * p):
                        pairs.append((i + j, i + j + k))
            k //= 2
        p *= 2
    return pairs


def _cmp_exchange(x, y):
    if x is None:
        return y, None
    if y is None:
        return x, None
    return jnp.maximum(x, y), jnp.minimum(x, y)


def _merge_top(xs, ys, n):
    xs = list(xs) + [None] * (n - len(xs))
    ys = list(ys) + [None] * (n - len(ys))
    out = [_cmp_exchange(xs[k], ys[n - 1 - k])[0] for k in range(n)]
    stride = n // 2
    while stride >= 1:
        for i in range(n):
            if i & stride == 0:
                out[i], out[i + stride] = _cmp_exchange(out[i], out[i + stride])
        stride //= 2
    return [v for v in out if v is not None]


def _top16_sorted(vals):
    groups = []
    for g in range(0, len(vals), PEER_TOPK):
        grp = list(vals[g:g + PEER_TOPK])
        for i, j in _oem_sort_pairs(PEER_TOPK):
            grp[i], grp[j] = _cmp_exchange(grp[i], grp[j])
        groups.append(grp)
    while len(groups) > 1:
        groups = [_merge_top(groups[i], groups[i + 1], PEER_TOPK) for i in range(0, len(groups), 2)]
    return groups[0]


def _peer_prep_kernel(x_ref, shift_ref, scale_ref, nw_ref, wqt_ref, keys_ref,
                      ht_ref, c1_ref, e1_ref, code2_ref, e2_ref, q_scr, s_scr, r_scr):
    h = _rms(x_ref[...], nw_ref[...]) * (1.0 + scale_ref[0]) + shift_ref[0]
    ht = h.T.astype(BF16)
    ht_ref[...] = ht
    tp = ht.shape[1]
    ntg = tp // LANES
    assert ntg == SUBLANES
    q_rows_per_dot = 4 * PEER_HALF_DIM
    for r0 in range(0, q_scr.shape[0], q_rows_per_dot):
        q_scr[r0:r0 + q_rows_per_dot, :] = _dot(wqt_ref[r0:r0 + q_rows_per_dot, :], ht).astype(BF16)

    def head(hd, carry):
        vals = []
        for k in range(2):
            q_rows = pl.ds(pl.multiple_of((hd * 2 + k) * PEER_HALF_DIM, PEER_HALF_DIM), PEER_HALF_DIM)
            st = _dot(keys_ref[k], q_scr[q_rows, :])
            for tg in range(ntg):
                s_scr[k, tg * PEER_KEYS:(tg + 1) * PEER_KEYS, :] = st[:, tg * LANES:(tg + 1) * LANES]
            vals.append([s_scr[k, pl.ds(key, ntg, stride=PEER_KEYS), :] for key in range(PEER_KEYS)])
        a = _top16_sorted(vals[0])
        b = _top16_sorted(vals[1])
        rows = [[a[i] + b[j] for j in range(PEER_TOPK // (i + 1))] for i in range(PEER_TOPK // 2)]
        col0 = [a[i] + b[0] for i in range(PEER_TOPK // 2, PEER_TOPK)]
        top = _merge_top(rows[0], _merge_top(rows[1], col0, PEER_TOPK), PEER_TOPK)
        rest = _merge_top(_merge_top(rows[2], rows[3], PEER_TOPK),
                          _merge_top(_merge_top(rows[4], rows[5], PEER_TOPK),
                                     _merge_top(rows[6], rows[7], PEER_TOPK), PEER_TOPK), PEER_TOPK)
        top = _merge_top(top, rest, PEER_TOPK)
        thr = top[-1]
        zsum = sum(jnp.exp(t - top[0]) for t in top[1:]) + 1.0
        half_inv_z = 0.5 / zsum
        none = float(PEER_TOPK + 1)
        grid_rows = rows + [[v] for v in col0]
        above = [sum(jnp.where(v > thr, 1.0, 0.0) for v in row) for row in grid_rows]
        tied = [sum(jnp.where(v == thr, 1.0, 0.0) for v in row) for row in grid_rows]
        left = float(PEER_TOPK) - sum(above)
        c_rank = []
        for i in range(PEER_TOPK):
            take = jnp.minimum(tied[i], left)
            left = left - take
            c_rank.append(none - above[i] - take)
        for r in range(PEER_TOPK):
            r_scr[r] = a[r]
            r_scr[PEER_TOPK + r] = b[r]
            r_scr[2 * PEER_TOPK + r] = c_rank[r]
        r_scr[3 * PEER_TOPK] = half_inv_z
        for tg in range(ntg):
            row = lambda idx: r_scr[idx, tg:tg + 1, :]
            rs = slice(tg * PEER_KEYS, (tg + 1) * PEER_KEYS)
            ls = slice(tg * LANES, (tg + 1) * LANES)
            s1 = s_scr[0, rs, :]
            c1 = jnp.full_like(s1, none)
            for r in reversed(range(PEER_TOPK)):
                c1 = jnp.where(s1 >= row(r), row(2 * PEER_TOPK + r), c1)
            c1_ref[hd, tg] = c1
            e1_ref[hd, tg] = jnp.exp(s1 - row(0)) * row(3 * PEER_TOPK)
            s2 = s_scr[1, rs, :]
            code2 = jnp.zeros_like(s2)
            for r in reversed(range(PEER_TOPK)):
                code2 = jnp.where(s2 >= row(PEER_TOPK + r), float(PEER_TOPK - r), code2)
            code2_ref[hd, :, ls] = code2.astype(BF16)
            e2_ref[hd, :, ls] = jnp.exp(s2 - row(PEER_TOPK)).astype(BF16)
        return carry

    lax.fori_loop(0, PEER_HEADS, head, 0)


def _peer_prep_call(x1, shift, scale, norm_w, w_qt, keys, tp, seq):
    t, d = x1.shape
    per_seq = seq // tp
    full = lambda a: pl.BlockSpec(a.shape, lambda i: (0,) * a.ndim)
    per_b = pl.BlockSpec((1, 1, d), lambda i: (i // per_seq, 0, 0))
    hk = pl.BlockSpec((PEER_HEADS, PEER_KEYS, tp), lambda i: (0, 0, i))
    hk_shape = lambda dt: jax.ShapeDtypeStruct((PEER_HEADS, PEER_KEYS, t), dt)
    hr = pl.BlockSpec((PEER_HEADS, tp // LANES, PEER_KEYS, LANES), lambda i: (0, i, 0, 0))
    hr_shape = jax.ShapeDtypeStruct((PEER_HEADS, t // LANES, PEER_KEYS, LANES), F32)
    rows = (tp // LANES) * PEER_KEYS
    return pl.pallas_call(
        _peer_prep_kernel,
        grid=(t // tp,),
        in_specs=[pl.BlockSpec((tp, d), lambda i: (i, 0)), per_b, per_b, full(norm_w),
                  pl.BlockSpec(w_qt.shape, lambda i: (0, 0), pipeline_mode=pl.Buffered(1)),
                  full(keys)],
        out_specs=(pl.BlockSpec((d, tp), lambda i: (0, i)), hr, hr, hk, hk),
        out_shape=(jax.ShapeDtypeStruct((d, t), BF16), hr_shape, hr_shape,
                   hk_shape(BF16), hk_shape(BF16)),
        scratch_shapes=[pltpu.VMEM((w_qt.shape[0], tp), BF16), pltpu.VMEM((2, rows, LANES), F32),
                        pltpu.VMEM((3 * PEER_TOPK + 1, tp // LANES, LANES), F32)],
        compiler_params=_cparams(1),
        name="peer_prep",
    )(x1, shift, scale, norm_w, w_qt, keys)


def _gelu_x2(x):
    k = math.sqrt(2.0 / math.pi)
    inner = x * (x * x * (0.044715 * k) + k)
    return x * jnp.tanh(inner) + x


def _peer_dense_kernel(n_eb, ht_ref, u_ref, vt_ref, c1_ref, e1_ref, code2_ref, e2_ref, x_ref,
                       gate_ref, fnw_ref, o_ref, acc_ref, act_a_ref, act_b_ref, wg_ref):
    s = pl.program_id(0)
    jv = lax.rem(jnp.maximum(s - 1, 0), n_eb)
    slot = lax.rem(s, 2)

    @pl.when(s == 0)
    def _():
        act_b_ref[...] = jnp.zeros_like(act_b_ref)

    @pl.when(jv == 0)
    def _():
        acc_ref[...] = jnp.zeros_like(acc_ref)

    tb = act_a_ref.shape[1]
    n_tiles = PEER_KEYS // BF16_ROWS

    def step(act_new, act_prev):
        per_dot = ACT_DOT_ROWS // PEER_KEYS
        for i1 in range(c1_ref.shape[2]):
            rows = slice(i1 * PEER_KEYS, (i1 + 1) * PEER_KEYS)
            if i1 % per_dot == 0:
                dot_rows = slice(i1 * PEER_KEYS, i1 * PEER_KEYS + ACT_DOT_ROWS)
                act_rows = _gelu_x2(_dot(u_ref[dot_rows, :], ht_ref[...])).astype(BF16)
            bcast = pl.ds(i1, BF16_ROWS, stride=0)
            for g0 in range(0, tb // LANES, GATE_LANE_GROUPS):
                lts = range(g0, g0 + GATE_LANE_GROUPS)
                ls = slice(g0 * LANES, (g0 + GATE_LANE_GROUPS) * LANES)
                w = [None] * n_tiles
                for hd in range(PEER_HEADS):
                    c_row = jnp.concatenate([c1_ref[hd, lt, bcast, :] for lt in lts],
                                            axis=1).astype(BF16)
                    e_row = jnp.concatenate([e1_ref[hd, lt, bcast, :] for lt in lts],
                                            axis=1).astype(BF16)
                    for it in range(n_tiles):
                        rs = slice(it * BF16_ROWS, (it + 1) * BF16_ROWS)
                        sel = code2_ref[hd, rs, ls] >= c_row
                        term = jnp.where(sel, e2_ref[hd, rs, ls] * e_row, jnp.zeros_like(e_row))
                        w[it] = term if w[it] is None else w[it] + term
                wg_ref[rows, ls] = jnp.concatenate(w, axis=0) * act_prev[rows, ls]
            if i1 % per_dot == per_dot - 1:
                act_new[dot_rows, :] = act_rows
        acc_ref[...] += _dot(vt_ref[0], wg_ref[...])

    pl.when(slot == 0)(lambda: step(act_a_ref, act_b_ref))
    pl.when(slot == 1)(lambda: step(act_b_ref, act_a_ref))

    @pl.when(jnp.logical_and(s > 0, jv == n_eb - 1))
    def _():
        x2 = x_ref[...] + gate_ref[0] * acc_ref[...].T
        o_ref[...] = _rms(x2, fnw_ref[...])


def _peer_dense_call(ht, u_b, vt_b, c1, e1, code2, e2, x1, gate, fnw, tb, eb, seq):
    d, t = ht.shape
    e = u_b.shape[0]
    per_seq = seq // tb
    n_i1 = eb // PEER_KEYS
    n_eb = e // eb
    total = (t // tb) * n_eb
    ti1 = lambda s: jnp.minimum(s, total - 1) // n_eb
    ej1 = lambda s: jnp.minimum(s, total - 1) % n_eb
    ti2 = lambda s: jnp.maximum(s - 1, 0) // n_eb
    ej2 = lambda s: jnp.maximum(s - 1, 0) % n_eb
    rows1 = pl.BlockSpec((PEER_HEADS, tb // LANES, n_i1, LANES), lambda s: (0, ti2(s), ej2(s), 0))
    rows2 = pl.BlockSpec((PEER_HEADS, PEER_KEYS, tb), lambda s: (0, 0, ti2(s)))
    return pl.pallas_call(
        functools.partial(_peer_dense_kernel, n_eb),
        grid=(total + 1,),
        in_specs=[pl.BlockSpec((d, tb), lambda s: (0, ti1(s))),
                  pl.BlockSpec((eb, d), lambda s: (ej1(s), 0)),
                  pl.BlockSpec((1, d, eb), lambda s: (ej2(s), 0, 0)),
                  rows1, rows1, rows2, rows2,
                  pl.BlockSpec((tb, d), lambda s: (ti2(s), 0), pipeline_mode=pl.Buffered(1)),
                  pl.BlockSpec((1, 1, d), lambda s: (ti2(s) // per_seq, 0, 0)),
                  pl.BlockSpec((1, d), lambda s: (0, 0))],
        out_specs=pl.BlockSpec((tb, d), lambda s: (ti2(s), 0)),
        out_shape=jax.ShapeDtypeStruct((t, d), F32),
        scratch_shapes=[pltpu.VMEM((d, tb), F32), pltpu.VMEM((eb, tb), BF16),
                        pltpu.VMEM((eb, tb), BF16), pltpu.VMEM((eb, tb), BF16)],
        compiler_params=_cparams(1),
        name="peer_dense",
    )(ht, u_b, vt_b, c1, e1, code2, e2, x1, gate, fnw)


def _tile_sizes(seq):
    tm = min(512, seq)
    return tm, SUBLANES * LANES, min(512, seq), 2048


def kernel(x, c, ctx, c_ctx, w_mod, b_mod, norm1_w, w_in, conv_w, conv_b, dt_bias, a_log, d_skip,
           ssd_norm_w, gmlp_norm_w, gmlp_ws, gmlp_bs, w_out, norm2_w, peer_wq, peer_keys, peer_u,
           peer_v, final_norm_w):
    b, seq, d = x.shape
    assert w_mod.shape[0] == 1, "single-layer block"
    assert seq % CHUNK == 0 and ctx.shape[1] % CHUNK == 0 and b + 1 <= SUBLANES
    tm, tp, tb, eb = _tile_sizes(seq)
    assert seq % tm == 0 and seq % tp == 0 and seq % tb == 0 and seq % SSD_STEP_ROWS == 0

    cc = jnp.zeros((SUBLANES, d), F32).at[:b].set(c).at[b].set(c_ctx)
    mod = _mod_call(cc, w_mod[0], b_mod[0][None, :]).reshape(SUBLANES, 6, d)
    mod_x = [mod[:b, k][:, None, :] for k in range(6)]
    shift_s, scale_s = mod[b, 0][None, :], mod[b, 1][None, :]

    nf = d // 4
    omega = (1.0 / (10000.0 ** (jnp.arange(nf, dtype=F32) / nf)))[None, :]
    pos_row, pos_col = _pos_call(seq, d, omega)

    w = w_in[0]
    o1, o2 = SSD_WIDTH, SSD_WIDTH + CONV_CH
    o3 = o2 + 2 * SSD_HEADS
    dt_pad = jnp.zeros((d, LANES - 2 * SSD_HEADS), F32)
    w_dt = jnp.concatenate([w[:, o2:o3], dt_pad], axis=1)
    w_main = jnp.concatenate([w[:, :o2], w[:, o3:], w_dt], axis=1).astype(BF16)
    w_xd = jnp.concatenate([w[:, o1:o2], w_dt], axis=1).astype(BF16)
    pad1 = jnp.zeros((1, LANES - 2 * SSD_HEADS), F32)
    dtb = jnp.concatenate([dt_bias[0].reshape(1, -1), pad1], axis=1)
    alog = jnp.concatenate([a_log[0].reshape(1, -1), pad1], axis=1)
    dskip = jnp.repeat(d_skip[0], SSD_HEAD_DIM)[None, :]
    cw, cb = conv_w[0], conv_b[0][None, :]
    n1 = norm1_w[0][None, :]

    st_f, st_b = _ctx_call(ctx, shift_s, scale_s, n1, w_xd, cw, cb, dtb, alog)

    gw = gmlp_ws[0].astype(BF16)
    gnw = gmlp_norm_w[0].reshape(1, GMLP_WIDTH)
    gb = jnp.repeat(gmlp_bs[0].T, GMLP_GROUP_DIM, axis=1)
    xp, z, xbc_raw, dt_raw, y_g = _inproj_call(x, pos_row, pos_col, mod_x[0], mod_x[1], n1, w_main, gw,
                                               gnw, gb, tm)

    y_f = _ssd_fwd_call(xbc_raw, dt_raw, cw, cb, dtb, alog, dskip, st_f)
    x1 = _ssd_bwd_call(xbc_raw, dt_raw, cw, cb, dtb, alog, st_b, y_f, z, y_g, xp, mod_x[2],
                       ssd_norm_w[0][None, :], w_out[0].astype(BF16))

    x1f = x1.reshape(b * seq, d)
    ht, c1, e1, code2, e2 = _peer_prep_call(
        x1f, mod_x[3], mod_x[4], norm2_w[0][None, :], peer_wq[0].astype(BF16).T,
        peer_keys[0].astype(BF16), tp, seq)
    vt_blocks = peer_v[0].astype(BF16).reshape(-1, eb, d).transpose(0, 2, 1)
    out = _peer_dense_call(ht, peer_u[0].astype(BF16), vt_blocks, c1, e1, code2, e2,
                           x1f, mod_x[5], final_norm_w[None, :], tb, eb, seq)
    return out.reshape(b, seq, d)
```

```python
import functools
import math

import jax
import jax.numpy as jnp
from jax import lax
from jax.experimental import pallas as pl
from jax.experimental.pallas import tpu as pltpu

F32 = jnp.float32
BF16 = jnp.bfloat16

GRID_W = 64
SSD_HEAD_DIM = 64
SSD_HEADS = 16
SSD_GROUPS = 2
HEADS_PER_GROUP = SSD_HEADS // SSD_GROUPS
SSD_STATE = 128
CHUNK = 128
SSD_WIDTH = SSD_HEADS * SSD_HEAD_DIM
GN = SSD_GROUPS * SSD_STATE
CONV_CH = SSD_WIDTH + 2 * GN
GMLP_GROUPS = 8
GMLP_GROUP_DIM = 128
GMLP_WIDTH = GMLP_GROUPS * GMLP_GROUP_DIM
PEER_KEYS = 128
PEER_HEADS = 8
PEER_HALF_DIM = 128
PEER_TOPK = 16
EPS = 1e-6

LANES = 128
SUBLANES = 8
BF16_ROWS = 16
VMEM_LIMIT = 56 * 1024 * 1024

GROUP_W = HEADS_PER_GROUP * SSD_HEAD_DIM
SSD_STEP_ROWS = 4 * CHUNK
GATE_LANE_GROUPS = 4
ACT_DOT_ROWS = 128


PAR, SEQ = "parallel", "arbitrary"


def _cparams(*semantics):
    return pltpu.CompilerParams(dimension_semantics=semantics, vmem_limit_bytes=VMEM_LIMIT)


def _rms(x, w):
    return x * lax.rsqrt(jnp.mean(x * x, axis=-1, keepdims=True) + EPS) * w


def _silu(x):
    return x * jax.nn.sigmoid(x)


def _dot(a, b):
    return jnp.dot(a, b, preferred_element_type=F32)


def _dot_nt(a, b):
    return lax.dot_general(a, b, (((1,), (1,)), ((), ())), preferred_element_type=F32)


def _mod_kernel(c_ref, w_ref, b_ref, o_ref):
    a = _silu(c_ref[...]).astype(BF16)
    o_ref[...] = _dot(a, w_ref[...].astype(BF16)) + b_ref[...]


def _mod_call(cc, w_mod, b_mod):
    d, n = w_mod.shape
    bn = 1536
    return pl.pallas_call(
        _mod_kernel,
        grid=(n // bn,),
        in_specs=[pl.BlockSpec((SUBLANES, d), lambda j: (0, 0)),
                  pl.BlockSpec((d, bn), lambda j: (0, j)),
                  pl.BlockSpec((1, bn), lambda j: (0, j))],
        out_specs=pl.BlockSpec((SUBLANES, bn), lambda j: (0, j)),
        out_shape=jax.ShapeDtypeStruct((SUBLANES, n), F32),
        compiler_params=_cparams(PAR),
        name="mod",
    )(cc, w_mod, b_mod)


def _pos_kernel(omega_ref, row_ref, col_ref):
    nf = omega_ref.shape[1]
    om = omega_ref[...]
    for ref in (row_ref, col_ref):
        ang = lax.broadcasted_iota(jnp.int32, (ref.shape[0], nf), 0).astype(F32) * om
        ref[:, :nf] = jnp.sin(ang)
        ref[:, nf:] = jnp.cos(ang)


def _pos_call(seq, dim, omega):
    half = dim // 2
    return pl.pallas_call(
        _pos_kernel,
        out_shape=(jax.ShapeDtypeStruct((seq // GRID_W, half), F32),
                   jax.ShapeDtypeStruct((GRID_W, half), F32)),
        name="pos",
    )(omega)


def _conv_silu(x, prev_row, next_row, w, b):
    n = x.shape[0]
    rows = lax.broadcasted_iota(jnp.int32, x.shape, 0)
    x_prev = jnp.where(rows == 0, prev_row, pltpu.roll(x, 1, axis=0))
    x_next = jnp.where(rows == n - 1, next_row, pltpu.roll(x, n - 1, axis=0))
    y = b + x_prev * w[0:1, :] + x * w[1:2, :] + x_next * w[2:3, :]
    return _silu(y)


def _softplus(x):
    return jnp.maximum(x, 0.0) + jnp.log1p(jnp.exp(-jnp.abs(x)))


def _split3(x):
    hi = x.astype(BF16)
    r = x - hi.astype(F32)
    mid = r.astype(BF16)
    return hi, mid, (r - mid.astype(F32)).astype(BF16)


def _dot_exact_rhs(x, m):
    return sum(_dot(p, m) for p in _split3(x))


def _dot_exact_lhs(m, x):
    return sum(_dot(m, p) for p in _split3(x))


def _ssd_chunk(xbc, dt_all, a_all, expand, h_ref, reverse, want_y):
    q = xbc.shape[0]
    col0 = SSD_HEADS if reverse else 0
    xs = xbc[:, :SSD_WIDTH]
    d_a = dt_all * a_all
    ti = lax.broadcasted_iota(jnp.int32, (q, q), 0)
    si = lax.broadcasted_iota(jnp.int32, (q, q), 1)
    keep = (si >= ti) if reverse else (si <= ti)
    tri = jnp.where(keep, 1.0, 0.0).astype(BF16)
    acs = _dot_exact_lhs(tri, d_a)
    last = 0 if reverse else q - 1
    total = acs[last:last + 1, :]
    stacked = jnp.concatenate([jnp.exp(acs), dt_all * jnp.exp(total - acs)], axis=0)
    ex = _dot_exact_rhs(stacked, expand)
    ea_x = ex[:q]
    xd = (xs * ex[q:]).astype(BF16)
    xs_b = xs.astype(BF16)
    chunk_decay = ea_x[last:last + 1, :]
    acs_t = acs.T if want_y else None
    dt_t = dt_all.T if want_y else None
    lane = lax.broadcasted_iota(jnp.int32, (q, LANES), 1)
    ys = []
    for g in range(SSD_GROUPS):
        bm = xbc[:, SSD_WIDTH + g * SSD_STATE:SSD_WIDTH + (g + 1) * SSD_STATE]
        cm = xbc[:, SSD_WIDTH + GN + g * SSD_STATE:SSD_WIDTH + GN + (g + 1) * SSD_STATE]
        gsl = slice(g * GROUP_W, (g + 1) * GROUP_W)
        h_old = h_ref[g]
        if want_y:
            cm_b = cm.astype(BF16)
            cb = _dot_nt(cm_b, bm.astype(BF16))
            y_off = _dot(cm_b, h_old.astype(BF16)) * ea_x[:, gsl]
            pieces = []
            for pr in range(HEADS_PER_GROUP // 2):
                ms = []
                for r in (2 * pr, 2 * pr + 1):
                    j = col0 + g * HEADS_PER_GROUP + r
                    seg = acs[:, j:j + 1] - acs_t[j:j + 1, :]
                    lm = jnp.exp(jnp.where(keep, seg, -jnp.inf))
                    ms.append((cb * lm * dt_t[j:j + 1, :]).astype(BF16))
                c0 = g * GROUP_W + pr * LANES
                xp = xs_b[:, c0:c0 + LANES]
                zero = jnp.zeros_like(xp)
                rhs = jnp.concatenate([jnp.where(lane < SSD_HEAD_DIM, xp, zero),
                                       jnp.where(lane >= SSD_HEAD_DIM, xp, zero)], axis=0)
                pieces.append(_dot(jnp.concatenate(ms, axis=1), rhs))
            ys.append(jnp.concatenate(pieces, axis=1) + y_off)
        h_ref[g] = h_old * chunk_decay[:, gsl] + _dot(bm.T.astype(BF16), xd[:, gsl])
    return jnp.concatenate(ys, axis=1) if want_y else None


def _expand_matrix(reverse):
    rows = lax.broadcasted_iota(jnp.int32, (LANES, SSD_WIDTH), 0)
    cols = lax.broadcasted_iota(jnp.int32, (LANES, SSD_WIDTH), 1)
    head = cols // SSD_HEAD_DIM + (SSD_HEADS if reverse else 0)
    return jnp.where(rows == head, 1.0, 0.0).astype(BF16)


def _ctx_kernel(ctx_ref, shift_ref, scale_ref, nw_ref, w_ref, cw_ref, cb_ref, dtb_ref, alog_ref,
                stf_ref, stb_ref):
    h = _rms(ctx_ref[0], nw_ref[...]) * (1.0 + scale_ref[...]) + shift_ref[...]
    proj = _dot(h.astype(BF16), w_ref[...])
    n = proj.shape[0]
    zero_row = jnp.zeros((1, CONV_CH), F32)
    xbc = _conv_silu(proj[:, :CONV_CH], zero_row, zero_row, cw_ref[...], cb_ref[...])
    dt_all = _softplus(proj[:, CONV_CH:] + dtb_ref[...])
    a_all = -jnp.exp(alog_ref[...])
    stf_ref[...] = jnp.zeros_like(stf_ref)
    stb_ref[...] = jnp.zeros_like(stb_ref)
    nck = n // CHUNK
    ef, eb = _expand_matrix(False), _expand_matrix(True)
    for ci in range(nck):
        sl = slice(ci * CHUNK, (ci + 1) * CHUNK)
        _ssd_chunk(xbc[sl], dt_all[sl], a_all, ef, stf_ref.at[0], False, False)
    for ci in reversed(range(nck)):
        sl = slice(ci * CHUNK, (ci + 1) * CHUNK)
        _ssd_chunk(xbc[sl], dt_all[sl], a_all, eb, stb_ref.at[0], True, False)


def _ctx_call(ctx, shift, scale, norm_w, w_xd, conv_w, conv_b, dtb, alog):
    b, n, d = ctx.shape
    st_shape = jax.ShapeDtypeStruct((b, SSD_GROUPS, SSD_STATE, GROUP_W), F32)
    st_spec = pl.BlockSpec((1, SSD_GROUPS, SSD_STATE, GROUP_W), lambda i: (i, 0, 0, 0))
    full = lambda a: pl.BlockSpec(a.shape, lambda i: (0,) * a.ndim)
    return pl.pallas_call(
        _ctx_kernel,
        grid=(b,),
        in_specs=[pl.BlockSpec((1, n, d), lambda i: (i, 0, 0)),
                  full(shift), full(scale), full(norm_w), full(w_xd), full(conv_w), full(conv_b),
                  full(dtb), full(alog)],
        out_specs=(st_spec, st_spec),
        out_shape=(st_shape, st_shape),
        compiler_params=_cparams(PAR),
        name="ctx_state",
    )(ctx, shift, scale, norm_w, w_xd, conv_w, conv_b, dtb, alog)


def _inproj_kernel(x_ref, prow_ref, pcol_ref, shift_ref, scale_ref, nw_ref, w_ref, gw_ref, gnw_ref,
                   gb_ref, xp_ref, z_ref, xbc_ref, dt_ref, yg_ref):
    pcol = pcol_ref[...]
    pos = jnp.concatenate(
        [jnp.concatenate([jnp.broadcast_to(prow_ref[k:k + 1, :], pcol.shape), pcol], axis=1)
         for k in range(prow_ref.shape[0])], axis=0)
    xp = x_ref[0] + pos
    xp_ref[0] = xp
    h = _rms(xp, nw_ref[...]) * (1.0 + scale_ref[0]) + shift_ref[0]
    proj = _dot(h.astype(BF16), w_ref[...])
    z_ref[0] = proj[:, :SSD_WIDTH]
    xbc_ref[0] = proj[:, SSD_WIDTH:SSD_WIDTH + CONV_CH]
    o = SSD_WIDTH + CONV_CH
    u = jax.nn.gelu(proj[:, o:o + GMLP_WIDTH])
    v = jax.nn.gelu(proj[:, o + GMLP_WIDTH:o + 2 * GMLP_WIDTH])
    dt_ref[0] = proj[:, o + 2 * GMLP_WIDTH:]
    tm = u.shape[0]
    gnw = gnw_ref[...]
    gb = gb_ref[...]
    for g in range(GMLP_GROUPS):
        gs = slice(g * GMLP_GROUP_DIM, (g + 1) * GMLP_GROUP_DIM)
        vn = _rms(v[:, gs], gnw[:, gs]).astype(BF16)
        wg = gw_ref[g]
        for ci in range(tm // CHUNK):
            rs = slice(ci * CHUNK, (ci + 1) * CHUNK)
            mixed = _dot(wg, vn[rs]) + gb[:, gs]
            yg_ref[0, rs, gs] = (u[rs, gs] * mixed).astype(BF16)


def _inproj_call(x, pos_row, pos_col, shift, scale, norm_w, w_main, gw, gnw, gb, tm):
    b, seq, d = x.shape
    nt = seq // tm
    assert tm % (SUBLANES * GRID_W) == 0
    full = lambda a: pl.BlockSpec(a.shape, lambda i, j: (0,) * a.ndim)
    tok = lambda w: pl.BlockSpec((1, tm, w), lambda i, j: (i, j, 0))
    per_b = pl.BlockSpec((1, 1, d), lambda i, j: (i, 0, 0))
    shp = lambda w, dt: jax.ShapeDtypeStruct((b, seq, w), dt)
    return pl.pallas_call(
        _inproj_kernel,
        grid=(b, nt),
        in_specs=[tok(d), pl.BlockSpec((tm // GRID_W, d // 2), lambda i, j: (j, 0)), full(pos_col),
                  per_b, per_b, full(norm_w), full(w_main), full(gw), full(gnw), full(gb)],
        out_specs=(tok(d), tok(SSD_WIDTH), tok(CONV_CH), tok(LANES), tok(GMLP_WIDTH)),
        out_shape=(shp(d, F32), shp(SSD_WIDTH, F32), shp(CONV_CH, F32), shp(LANES, F32),
                   shp(GMLP_WIDTH, BF16)),
        compiler_params=_cparams(PAR, PAR),
        name="inproj",
    )(x, pos_row, pos_col, shift, scale, norm_w, w_main, gw, gnw, gb)


def _halo_rows(prev_ref, next_ref, c, nc):
    prev_row = jnp.where(c > 0, prev_ref[0, SUBLANES - 1:SUBLANES, :], 0.0)
    next_row = jnp.where(c < nc - 1, next_ref[0, 0:1, :], 0.0)
    return prev_row, next_row


def _ssd_fwd_kernel(xbc_ref, prev_ref, next_ref, dt_ref, cw_ref, cb_ref, dtb_ref, alog_ref,
                    dskip_ref, h0_ref, y_ref, h_ref):
    c = pl.program_id(1)
    nc = pl.num_programs(1)

    @pl.when(c == 0)
    def _():
        h_ref[...] = h0_ref[0]

    prev_row, next_row = _halo_rows(prev_ref, next_ref, c, nc)
    xbc = _conv_silu(xbc_ref[0], prev_row, next_row, cw_ref[...], cb_ref[...])
    dt_all = _softplus(dt_ref[0] + dtb_ref[...])
    a_all = -jnp.exp(alog_ref[...])
    expand = _expand_matrix(False)
    for ci in range(xbc.shape[0] // CHUNK):
        rs = slice(ci * CHUNK, (ci + 1) * CHUNK)
        y = _ssd_chunk(xbc[rs], dt_all[rs], a_all, expand, h_ref, False, True)
        y_ref[0, rs, :] = y + dskip_ref[...] * xbc[rs, :SSD_WIDTH]


def _ssd_bwd_kernel(xbc_ref, prev_ref, next_ref, dt_ref, cw_ref, cb_ref, dtb_ref, alog_ref,
                    h0_ref, yf_ref, z_ref, yg_ref, xp_ref, gate_ref, snw_ref, wo_ref,
                    o_ref, h_ref, ys_ref):
    cr = pl.program_id(1)
    nc = pl.num_programs(1)
    c = nc - 1 - cr

    @pl.when(cr == 0)
    def _():
        h_ref[...] = h0_ref[0]

    prev_row, next_row = _halo_rows(prev_ref, next_ref, c, nc)
    xbc = _conv_silu(xbc_ref[0], prev_row, next_row, cw_ref[...], cb_ref[...])
    dt_all = _softplus(dt_ref[0] + dtb_ref[...])
    a_all = -jnp.exp(alog_ref[...])
    expand = _expand_matrix(True)
    for ci in reversed(range(xbc.shape[0] // CHUNK)):
        rs = slice(ci * CHUNK, (ci + 1) * CHUNK)
        y = yf_ref[0, rs, :] + _ssd_chunk(xbc[rs], dt_all[rs], a_all, expand, h_ref, True, True)
        ys_ref[rs, :] = _rms(y * _silu(z_ref[0, rs, :]), snw_ref[...]).astype(BF16)
    out = _dot(ys_ref[...], wo_ref[:SSD_WIDTH, :]) + _dot(yg_ref[0], wo_ref[SSD_WIDTH:, :])
    o_ref[0] = xp_ref[0] + gate_ref[0] * out


def _ssd_specs(seq, reverse):
    rows = min(SSD_STEP_ROWS, seq)
    nc = seq // rows
    hb = rows // SUBLANES
    last_hb = seq // SUBLANES - 1
    cidx = (lambda j: nc - 1 - j) if reverse else (lambda j: j)
    cur = lambda w: pl.BlockSpec((1, rows, w), lambda i, j: (i, cidx(j), 0))
    prev = pl.BlockSpec((1, SUBLANES, CONV_CH),
                        lambda i, j: (i, jnp.maximum(cidx(j) * hb - 1, 0), 0))
    nxt = pl.BlockSpec((1, SUBLANES, CONV_CH),
                       lambda i, j: (i, jnp.minimum((cidx(j) + 1) * hb, last_hb), 0))
    return nc, cur, prev, nxt


def _ssd_fwd_call(xbc_raw, dt_raw, conv_w, conv_b, dtb, alog, dskip, st_f):
    b, seq, _ = xbc_raw.shape
    nc, cur, prev, nxt = _ssd_specs(seq, False)
    full = lambda a: pl.BlockSpec(a.shape, lambda i, j: (0,) * a.ndim)
    st = pl.BlockSpec((1, SSD_GROUPS, SSD_STATE, GROUP_W), lambda i, j: (i, 0, 0, 0))
    return pl.pallas_call(
        _ssd_fwd_kernel,
        grid=(b, nc),
        in_specs=[cur(CONV_CH), prev, nxt, cur(LANES), full(conv_w), full(conv_b), full(dtb),
                  full(alog), full(dskip), st],
        out_specs=cur(SSD_WIDTH),
        out_shape=jax.ShapeDtypeStruct((b, seq, SSD_WIDTH), F32),
        scratch_shapes=[pltpu.VMEM((SSD_GROUPS, SSD_STATE, GROUP_W), F32)],
        compiler_params=_cparams(PAR, SEQ),
        name="ssd_fwd",
    )(xbc_raw, xbc_raw, xbc_raw, dt_raw, conv_w, conv_b, dtb, alog, dskip, st_f)


def _ssd_bwd_call(xbc_raw, dt_raw, conv_w, conv_b, dtb, alog, st_b, y_f, z, y_g, xp, gate, snw, w_out):
    b, seq, d = xp.shape
    nc, cur, prev, nxt = _ssd_specs(seq, True)
    full = lambda a: pl.BlockSpec(a.shape, lambda i, j: (0,) * a.ndim)
    st = pl.BlockSpec((1, SSD_GROUPS, SSD_STATE, GROUP_W), lambda i, j: (i, 0, 0, 0))
    per_b = pl.BlockSpec((1, 1, d), lambda i, j: (i, 0, 0))
    return pl.pallas_call(
        _ssd_bwd_kernel,
        grid=(b, nc),
        in_specs=[cur(CONV_CH), prev, nxt, cur(LANES), full(conv_w), full(conv_b), full(dtb),
                  full(alog), st, cur(SSD_WIDTH), cur(SSD_WIDTH), cur(GMLP_WIDTH), cur(d), per_b,
                  full(snw), full(w_out)],
        out_specs=cur(d),
        out_shape=jax.ShapeDtypeStruct((b, seq, d), F32),
        scratch_shapes=[pltpu.VMEM((SSD_GROUPS, SSD_STATE, GROUP_W), F32),
                        pltpu.VMEM((min(SSD_STEP_ROWS, seq), SSD_WIDTH), BF16)],
        compiler_params=_cparams(PAR, SEQ),
        name="ssd_bwd",
    )(xbc_raw, xbc_raw, xbc_raw, dt_raw, conv_w, conv_b, dtb, alog, st_b, y_f, z, y_g, xp, gate,
      snw, w_out)


def _oem_sort_pairs(n):
    pairs = []
    p = 1
    while p < n:
        k = p
        while k >= 1:
            for j in range(k % p, n - k, 2 * k):
                for i in range(min(k, n - j - k)):
                    if (i + j) // (2 * p) == (i + j + k) // (2 * p):
                        pairs.append((i + j, i + j + k))
            k //= 2
        p *= 2
    return pairs


def _cmp_exchange(x, y):
    if x is None:
        return y, None
    if y is None:
        return x, None
    return jnp.maximum(x, y), jnp.minimum(x, y)


def _merge_top(xs, ys, n):
    xs = list(xs) + [None] * (n - len(xs))
    ys = list(ys) + [None] * (n - len(ys))
    out = [_cmp_exchange(xs[k], ys[n - 1 - k])[0] for k in range(n)]
    stride = n // 2
    while stride >= 1:
        for i in range(n):
            if i & stride == 0:
                out[i], out[i + stride] = _cmp_exchange(out[i], out[i + stride])
        stride //= 2
    return [v for v in out if v is not None]


def _top16_sorted(vals):
    groups = []
    for g in range(0, len(vals), PEER_TOPK):
        grp = list(vals[g:g + PEER_TOPK])
        for i, j in _oem_sort_pairs(PEER_TOPK):
            grp[i], grp[j] = _cmp_exchange(grp[i], grp[j])
        groups.append(grp)
    while len(groups) > 1:
        groups = [_merge_top(groups[i], groups[i + 1], PEER_TOPK) for i in range(0, len(groups), 2)]
    return groups[0]


def _peer_prep_kernel(x_ref, shift_ref, scale_ref, nw_ref, wqt_ref, keys_ref,
                      ht_ref, c1_ref, e1_ref, code2_ref, e2_ref, q_scr, s_scr, r_scr):
    h = _rms(x_ref[...], nw_ref[...]) * (1.0 + scale_ref[0]) + shift_ref[0]
    ht = h.T.astype(BF16)
    ht_ref[...] = ht
    tp = ht.shape[1]
    ntg = tp // LANES
    assert ntg == SUBLANES
    q_rows_per_dot = 4 * PEER_HALF_DIM
    for r0 in range(0, q_scr.shape[0], q_rows_per_dot):
        q_scr[r0:r0 + q_rows_per_dot, :] = _dot(wqt_ref[r0:r0 + q_rows_per_dot, :], ht).astype(BF16)

    def head(hd, carry):
        vals = []
        for k in range(2):
            q_rows = pl.ds(pl.multiple_of((hd * 2 + k) * PEER_HALF_DIM, PEER_HALF_DIM), PEER_HALF_DIM)
            st = _dot(keys_ref[k], q_scr[q_rows, :])
            for tg in range(ntg):
                s_scr[k, tg * PEER_KEYS:(tg + 1) * PEER_KEYS, :] = st[:, tg * LANES:(tg + 1) * LANES]
            vals.append([s_scr[k, pl.ds(key, ntg, stride=PEER_KEYS), :] for key in range(PEER_KEYS)])
        a = _top16_sorted(vals[0])
        b = _top16_sorted(vals[1])
        rows = [[a[i] + b[j] for j in range(PEER_TOPK // (i + 1))] for i in range(PEER_TOPK // 2)]
        col0 = [a[i] + b[0] for i in range(PEER_TOPK // 2, PEER_TOPK)]
        top = _merge_top(rows[0], _merge_top(rows[1], col0, PEER_TOPK), PEER_TOPK)
        rest = _merge_top(_merge_top(rows[2], rows[3], PEER_TOPK),
                          _merge_top(_merge_top(rows[4], rows[5], PEER_TOPK),
                                     _merge_top(rows[6], rows[7], PEER_TOPK), PEER_TOPK), PEER_TOPK)
        top = _merge_top(top, rest, PEER_TOPK)
        thr = top[-1]
        zsum = sum(jnp.exp(t - top[0]) for t in top[1:]) + 1.0
        half_inv_z = 0.5 / zsum
        none = float(PEER_TOPK + 1)
        grid_rows = rows + [[v] for v in col0]
        above = [sum(jnp.where(v > thr, 1.0, 0.0) for v in row) for row in grid_rows]
        tied = [sum(jnp.where(v == thr, 1.0, 0.0) for v in row) for row in grid_rows]
        left = float(PEER_TOPK) - sum(above)
        c_rank = []
        for i in range(PEER_TOPK):
            take = jnp.minimum(tied[i], left)
            left = left - take
            c_rank.append(none - above[i] - take)
        for r in range(PEER_TOPK):
            r_scr[r] = a[r]
            r_scr[PEER_TOPK + r] = b[r]
            r_scr[2 * PEER_TOPK + r] = c_rank[r]
        r_scr[3 * PEER_TOPK] = half_inv_z
        for tg in range(ntg):
            row = lambda idx: r_scr[idx, tg:tg + 1, :]
            rs = slice(tg * PEER_KEYS, (tg + 1) * PEER_KEYS)
            ls = slice(tg * LANES, (tg + 1) * LANES)
            s1 = s_scr[0, rs, :]
            c1 = jnp.full_like(s1, none)
            for r in reversed(range(PEER_TOPK)):
                c1 = jnp.where(s1 >= row(r), row(2 * PEER_TOPK + r), c1)
            c1_ref[hd, tg] = c1
            e1_ref[hd, tg] = jnp.exp(s1 - row(0)) * row(3 * PEER_TOPK)
            s2 = s_scr[1, rs, :]
            code2 = jnp.zeros_like(s2)
            for r in reversed(range(PEER_TOPK)):
                code2 = jnp.where(s2 >= row(PEER_TOPK + r), float(PEER_TOPK - r), code2)
            code2_ref[hd, :, ls] = code2.astype(BF16)
            e2_ref[hd, :, ls] = jnp.exp(s2 - row(PEER_TOPK)).astype(BF16)
        return carry

    lax.fori_loop(0, PEER_HEADS, head, 0)


def _peer_prep_call(x1, shift, scale, norm_w, w_qt, keys, tp, seq):
    t, d = x1.shape
    per_seq = seq // tp
    full = lambda a: pl.BlockSpec(a.shape, lambda i: (0,) * a.ndim)
    per_b = pl.BlockSpec((1, 1, d), lambda i: (i // per_seq, 0, 0))
    hk = pl.BlockSpec((PEER_HEADS, PEER_KEYS, tp), lambda i: (0, 0, i))
    hk_shape = lambda dt: jax.ShapeDtypeStruct((PEER_HEADS, PEER_KEYS, t), dt)
    hr = pl.BlockSpec((PEER_HEADS, tp // LANES, PEER_KEYS, LANES), lambda i: (0, i, 0, 0))
    hr_shape = jax.ShapeDtypeStruct((PEER_HEADS, t // LANES, PEER_KEYS, LANES), F32)
    rows = (tp // LANES) * PEER_KEYS
    return pl.pallas_call(
        _peer_prep_kernel,
        grid=(t // tp,),
        in_specs=[pl.BlockSpec((tp, d), lambda i: (i, 0)), per_b, per_b, full(norm_w),
                  pl.BlockSpec(w_qt.shape, lambda i: (0, 0), pipeline_mode=pl.Buffered(1)),
                  full(keys)],
        out_specs=(pl.BlockSpec((d, tp), lambda i: (0, i)), hr, hr, hk, hk),
        out_shape=(jax.ShapeDtypeStruct((d, t), BF16), hr_shape, hr_shape,
                   hk_shape(BF16), hk_shape(BF16)),
        scratch_shapes=[pltpu.VMEM((w_qt.shape[0], tp), BF16), pltpu.VMEM((2, rows, LANES), F32),
                        pltpu.VMEM((3 * PEER_TOPK + 1, tp // LANES, LANES), F32)],
        compiler_params=_cparams(PAR),
        name="peer_prep",
    )(x1, shift, scale, norm_w, w_qt, keys)


def _gelu_x2(x):
    k = math.sqrt(2.0 / math.pi)
    inner = x * (x * x * (0.044715 * k) + k)
    return x * jnp.tanh(inner) + x


def _peer_dense_kernel(n_eb, ht_ref, u_ref, vt_ref, c1_ref, e1_ref, code2_ref, e2_ref, x_ref,
                       gate_ref, fnw_ref, o_ref, acc_ref, act_a_ref, act_b_ref, wg_ref):
    s = pl.program_id(0)
    jv = lax.rem(jnp.maximum(s - 1, 0), n_eb)
    slot = lax.rem(s, 2)

    @pl.when(s == 0)
    def _():
        act_b_ref[...] = jnp.zeros_like(act_b_ref)

    @pl.when(jv == 0)
    def _():
        acc_ref[...] = jnp.zeros_like(acc_ref)

    tb = act_a_ref.shape[1]
    n_tiles = PEER_KEYS // BF16_ROWS

    def step(act_new, act_prev):
        per_dot = ACT_DOT_ROWS // PEER_KEYS
        for i1 in range(c1_ref.shape[2]):
            rows = slice(i1 * PEER_KEYS, (i1 + 1) * PEER_KEYS)
            if i1 % per_dot == 0:
                dot_rows = slice(i1 * PEER_KEYS, i1 * PEER_KEYS + ACT_DOT_ROWS)
                act_rows = _gelu_x2(_dot(u_ref[dot_rows, :], ht_ref[...])).astype(BF16)
            bcast = pl.ds(i1, BF16_ROWS, stride=0)
            for g0 in range(0, tb // LANES, GATE_LANE_GROUPS):
                lts = range(g0, g0 + GATE_LANE_GROUPS)
                ls = slice(g0 * LANES, (g0 + GATE_LANE_GROUPS) * LANES)
                w = [None] * n_tiles
                for hd in range(PEER_HEADS):
                    c_row = jnp.concatenate([c1_ref[hd, lt, bcast, :] for lt in lts],
                                            axis=1).astype(BF16)
                    e_row = jnp.concatenate([e1_ref[hd, lt, bcast, :] for lt in lts],
                                            axis=1).astype(BF16)
                    for it in range(n_tiles):
                        rs = slice(it * BF16_ROWS, (it + 1) * BF16_ROWS)
                        sel = code2_ref[hd, rs, ls] >= c_row
                        term = jnp.where(sel, e2_ref[hd, rs, ls] * e_row, jnp.zeros_like(e_row))
                        w[it] = term if w[it] is None else w[it] + term
                wg_ref[rows, ls] = jnp.concatenate(w, axis=0) * act_prev[rows, ls]
            if i1 % per_dot == per_dot - 1:
                act_new[dot_rows, :] = act_rows
        acc_ref[...] += _dot(vt_ref[0], wg_ref[...])

    pl.when(slot == 0)(lambda: step(act_a_ref, act_b_ref))
    pl.when(slot == 1)(lambda: step(act_b_ref, act_a_ref))

    @pl.when(jnp.logical_and(s > 0, jv == n_eb - 1))
    def _():
        x2 = x_ref[...] + gate_ref[0] * acc_ref[...].T
        o_ref[...] = _rms(x2, fnw_ref[...])


def _peer_dense_call(ht, u_b, vt_b, c1, e1, code2, e2, x1, gate, fnw, tb, eb, seq):
    d, t = ht.shape
    e = u_b.shape[0]
    per_seq = seq // tb
    n_i1 = eb // PEER_KEYS
    n_eb = e // eb
    total = (t // tb) * n_eb
    ti1 = lambda s: jnp.minimum(s, total - 1) // n_eb
    ej1 = lambda s: jnp.minimum(s, total - 1) % n_eb
    ti2 = lambda s: jnp.maximum(s - 1, 0) // n_eb
    ej2 = lambda s: jnp.maximum(s - 1, 0) % n_eb
    rows1 = pl.BlockSpec((PEER_HEADS, tb // LANES, n_i1, LANES), lambda s: (0, ti2(s), ej2(s), 0))
    rows2 = pl.BlockSpec((PEER_HEADS, PEER_KEYS, tb), lambda s: (0, 0, ti2(s)))
    return pl.pallas_call(
        functools.partial(_peer_dense_kernel, n_eb),
        grid=(total + 1,),
        in_specs=[pl.BlockSpec((d, tb), lambda s: (0, ti1(s))),
                  pl.BlockSpec((eb, d), lambda s: (ej1(s), 0)),
                  pl.BlockSpec((1, d, eb), lambda s: (ej2(s), 0, 0)),
                  rows1, rows1, rows2, rows2,
                  pl.BlockSpec((tb, d), lambda s: (ti2(s), 0), pipeline_mode=pl.Buffered(1)),
                  pl.BlockSpec((1, 1, d), lambda s: (ti2(s) // per_seq, 0, 0)),
                  pl.BlockSpec((1, d), lambda s: (0, 0))],
        out_specs=pl.BlockSpec((tb, d), lambda s: (ti2(s), 0)),
        out_shape=jax.ShapeDtypeStruct((t, d), F32),
        scratch_shapes=[pltpu.VMEM((d, tb), F32), pltpu.VMEM((eb, tb), BF16),
                        pltpu.VMEM((eb, tb), BF16), pltpu.VMEM((eb, tb), BF16)],
        compiler_params=_cparams(SEQ),
        name="peer_dense",
    )(ht, u_b, vt_b, c1, e1, code2, e2, x1, gate, fnw)


def _tile_sizes(seq):
    tm = min(512, seq)
    return tm, SUBLANES * LANES, min(512, seq), 2048


def kernel(x, c, ctx, c_ctx, w_mod, b_mod, norm1_w, w_in, conv_w, conv_b, dt_bias, a_log, d_skip,
           ssd_norm_w, gmlp_norm_w, gmlp_ws, gmlp_bs, w_out, norm2_w, peer_wq, peer_keys, peer_u,
           peer_v, final_norm_w):
    b, seq, d = x.shape
    assert w_mod.shape[0] == 1, "single-layer block"
    assert seq % CHUNK == 0 and ctx.shape[1] % CHUNK == 0 and b + 1 <= SUBLANES
    tm, tp, tb, eb = _tile_sizes(seq)
    assert seq % tm == 0 and seq % tp == 0 and seq % tb == 0 and seq % SSD_STEP_ROWS == 0

    cc = jnp.zeros((SUBLANES, d), F32).at[:b].set(c).at[b].set(c_ctx)
    mod = _mod_call(cc, w_mod[0], b_mod[0][None, :]).reshape(SUBLANES, 6, d)
    mod_x = [mod[:b, k][:, None, :] for k in range(6)]
    shift_s, scale_s = mod[b, 0][None, :], mod[b, 1][None, :]

    nf = d // 4
    omega = (1.0 / (10000.0 ** (jnp.arange(nf, dtype=F32) / nf)))[None, :]
    pos_row, pos_col = _pos_call(seq, d, omega)

    w = w_in[0]
    o1, o2 = SSD_WIDTH, SSD_WIDTH + CONV_CH
    o3 = o2 + 2 * SSD_HEADS
    dt_pad = jnp.zeros((d, LANES - 2 * SSD_HEADS), F32)
    w_dt = jnp.concatenate([w[:, o2:o3], dt_pad], axis=1)
    w_main = jnp.concatenate([w[:, :o2], w[:, o3:], w_dt], axis=1).astype(BF16)
    w_xd = jnp.concatenate([w[:, o1:o2], w_dt], axis=1).astype(BF16)
    pad1 = jnp.zeros((1, LANES - 2 * SSD_HEADS), F32)
    dtb = jnp.concatenate([dt_bias[0].reshape(1, -1), pad1], axis=1)
    alog = jnp.concatenate([a_log[0].reshape(1, -1), pad1], axis=1)
    dskip = jnp.repeat(d_skip[0], SSD_HEAD_DIM)[None, :]
    cw, cb = conv_w[0], conv_b[0][None, :]
    n1 = norm1_w[0][None, :]

    st_f, st_b = _ctx_call(ctx, shift_s, scale_s, n1, w_xd, cw, cb, dtb, alog)

    gw = gmlp_ws[0].astype(BF16)
    gnw = gmlp_norm_w[0].reshape(1, GMLP_WIDTH)
    gb = jnp.repeat(gmlp_bs[0].T, GMLP_GROUP_DIM, axis=1)
    xp, z, xbc_raw, dt_raw, y_g = _inproj_call(x, pos_row, pos_col, mod_x[0], mod_x[1], n1, w_main, gw,
                                               gnw, gb, tm)

    y_f = _ssd_fwd_call(xbc_raw, dt_raw, cw, cb, dtb, alog, dskip, st_f)
    x1 = _ssd_bwd_call(xbc_raw, dt_raw, cw, cb, dtb, alog, st_b, y_f, z, y_g, xp, mod_x[2],
                       ssd_norm_w[0][None, :], w_out[0].astype(BF16))

    x1f = x1.reshape(b * seq, d)
    ht, c1, e1, code2, e2 = _peer_prep_call(
        x1f, mod_x[3], mod_x[4], norm2_w[0][None, :], peer_wq[0].astype(BF16).T,
        peer_keys[0].astype(BF16), tp, seq)
    vt_blocks = peer_v[0].astype(BF16).reshape(-1, eb, d).transpose(0, 2, 1)
    out = _peer_dense_call(ht, peer_u[0].astype(BF16), vt_blocks, c1, e1, code2, e2,
                           x1f, mod_x[5], final_norm_w[None, :], tb, eb, seq)
    return out.reshape(b, seq, d)
```
